```python
import math
import jax, jax.numpy as jnp
from jax import lax
import numpy as np

D_MODEL = 1024
BATCH = 8
SEQ = 2048
DEPTH = 1
DEC_BATCH = 128
DEC_SEQ = 4
PAST_LEN = 8192
PAGE_SIZE = 128

MIX_WIDTH = D_MODEL
ATTN_WIDTH = MIX_WIDTH // 2
CONV_WIDTH = MIX_WIDTH - ATTN_WIDTH
N_HEADS = 4
V_HEAD_DIM = ATTN_WIDTH // N_HEADS
QK_HEAD_DIM = V_HEAD_DIM // 2
CONV_K = 3
D_FF = 4 * D_MODEL
Q_BLOCK = 128
EPS = 1e-6
NEG_INF = -1e30
W_IN_COLS = 3 * ATTN_WIDTH + 3 * CONV_WIDTH

kernel_name = "hymba_diffattn_shortconv_decode_step"


def rmsnorm(x, g):
    xf = x.astype(jnp.float32)
    r = lax.rsqrt(jnp.mean(xf * xf, axis=-1, keepdims=True) + EPS)
    return (xf * r * g.astype(jnp.float32)).astype(x.dtype)


def lambda_init_fn(layer):
    return 0.8 - 0.6 * math.exp(-0.3 * layer)


def mixer_inputs(x, norm_mix, w_in, q_norm, k_norm):
    h = rmsnorm(x, norm_mix)
    proj = h @ w_in
    cuts = np.cumsum([ATTN_WIDTH, ATTN_WIDTH, ATTN_WIDTH, CONV_WIDTH, CONV_WIDTH])
    q, k, v, b_gate, c_gate, xc = jnp.split(proj, list(cuts), axis=-1)
    lead = x.shape[:2]
    q = rmsnorm(q.reshape(*lead, N_HEADS, 2, QK_HEAD_DIM), q_norm)
    k = rmsnorm(k.reshape(*lead, N_HEADS, 2, QK_HEAD_DIM), k_norm)
    v = v.reshape(*lead, N_HEADS, V_HEAD_DIM)
    return q, k, v, b_gate, c_gate, xc


def diff_attend(q, k, v, mask, lam):
    scale = QK_HEAD_DIM ** -0.5
    s = jnp.einsum('bqhcd,bkhcd->bhcqk', q.astype(jnp.float32), k.astype(jnp.float32)) * scale
    s = jnp.where(mask, s, NEG_INF)
    p = jax.nn.softmax(s, axis=-1)
    pd = p[:, :, 0] - lam * p[:, :, 1]
    return jnp.einsum('bhqk,bkhd->bqhd', pd, v.astype(jnp.float32))


def prompt_attention(q, k, v, lam):
    b, s = q.shape[:2]
    nb = s // Q_BLOCK
    qb = q.reshape(b, nb, Q_BLOCK, N_HEADS, 2, QK_HEAD_DIM).swapaxes(0, 1)
    kpos = jnp.arange(s)

    def block(args):
        qi, i = args
        qpos = i * Q_BLOCK + jnp.arange(Q_BLOCK)
        mask = qpos[:, None] >= kpos[None, :]
        return diff_attend(qi, k, v, mask, lam)

    out = lax.map(block, (qb, jnp.arange(nb)))
    return out.swapaxes(0, 1).reshape(b, s, N_HEADS, V_HEAD_DIM)


def sample_attention(q, k_new, v_new, cache_k, cache_v, page_table, lam):
    past = page_table.shape[1] * cache_k.shape[1]
    t = q.shape[1]
    causal = jnp.arange(t)[:, None] >= jnp.arange(t)[None, :]
    scale = QK_HEAD_DIM ** -0.5

    def one(args):
        qi, ki, vi, pages = args
        kp = cache_k[pages].reshape(past, N_HEADS, 2, QK_HEAD_DIM)
        vp = cache_v[pages].reshape(past, N_HEADS, V_HEAD_DIM)
        qf = qi.astype(jnp.float32)
        s_past = jnp.einsum('qhcd,khcd->hcqk', qf, kp.astype(jnp.float32))
        s_new = jnp.einsum('qhcd,khcd->hcqk', qf, ki.astype(jnp.float32))
        s_new = jnp.where(causal, s_new, NEG_INF)
        s = jnp.concatenate([s_past, s_new], axis=-1) * scale
        p = jax.nn.softmax(s, axis=-1)
        pd = p[:, 0] - lam * p[:, 1]
        return (jnp.einsum('hqk,khd->qhd', pd[..., :past], vp.astype(jnp.float32))
                + jnp.einsum('hqk,khd->qhd', pd[..., past:], vi.astype(jnp.float32)))

    return lax.map(one, (q, k_new, v_new, page_table))


def attn_output(o, subln, lam_init, dtype):
    o = rmsnorm(o, subln) * (1.0 - lam_init)
    return o.reshape(*o.shape[:2], ATTN_WIDTH).astype(dtype)


def short_conv(b_gate, c_gate, xc, buf, conv_w):
    u = c_gate * xc
    padded = jnp.concatenate([buf.astype(u.dtype), u], axis=1)
    t = u.shape[1]
    y = conv_w[0] * padded[:, 0:t]
    for j in range(1, CONV_K):
        y = y + conv_w[j] * padded[:, j:j + t]
    return b_gate * y, padded[:, -(CONV_K - 1):]


def finish_layer(x, attn_o, conv_o, w_out, norm_mlp, w_ff1, w_ff2):
    x = x + jnp.concatenate([attn_o, conv_o], axis=-1) @ w_out
    h = rmsnorm(x, norm_mlp)
    return x + jnp.square(jax.nn.relu(h @ w_ff1)) @ w_ff2


def setup_inputs(seed: int = 0) -> dict:
    key = jax.random.key(seed)
    ks = jax.random.split(key, 24)
    f32 = jnp.float32
    n_pages = PAST_LEN // PAGE_SIZE
    n_used = DEC_BATCH * n_pages
    n_pool = n_used + n_used // 4

    def nrm(k, shape, scale):
        return jax.random.normal(k, shape, f32) * scale

    def gain(k, dim):
        return 1.0 + 0.01 * jax.random.normal(k, (DEPTH, dim), f32)

    page_table = jax.random.permutation(ks[5], n_pool)[:n_used].reshape(DEC_BATCH, n_pages).astype(jnp.int32)
    return {
        "x_prompt": nrm(ks[0], (BATCH, SEQ, D_MODEL), 1.0),
        "x_sample": nrm(ks[1], (DEC_BATCH, DEC_SEQ, D_MODEL), 1.0),
        "cache_k": nrm(ks[2], (DEPTH, n_pool, PAGE_SIZE, N_HEADS, 2, QK_HEAD_DIM), 1.0),
        "cache_v": nrm(ks[3], (DEPTH, n_pool, PAGE_SIZE, N_HEADS, V_HEAD_DIM), 1.0),
        "state_conv": nrm(ks[4], (DEPTH, DEC_BATCH, CONV_K - 1, CONV_WIDTH), 1.0),
        "page_table": page_table,
        "norm_mix": gain(ks[6], D_MODEL),
        "w_in": nrm(ks[7], (DEPTH, D_MODEL, W_IN_COLS), D_MODEL ** -0.5),
        "q_norm": gain(ks[8], QK_HEAD_DIM),
        "k_norm": gain(ks[9], QK_HEAD_DIM),
        "lambda_q1": nrm(ks[10], (DEPTH, QK_HEAD_DIM), 0.1),
        "lambda_k1": nrm(ks[11], (DEPTH, QK_HEAD_DIM), 0.1),
        "lambda_q2": nrm(ks[12], (DEPTH, QK_HEAD_DIM), 0.1),
        "lambda_k2": nrm(ks[13], (DEPTH, QK_HEAD_DIM), 0.1),
        "subln": gain(ks[14], V_HEAD_DIM),
        "conv_w": nrm(ks[15], (DEPTH, CONV_K, CONV_WIDTH), CONV_K ** -0.5),
        "w_out": nrm(ks[16], (DEPTH, MIX_WIDTH, D_MODEL), MIX_WIDTH ** -0.5),
        "norm_mlp": gain(ks[17], D_MODEL),
        "w_ff1": nrm(ks[18], (DEPTH, D_MODEL, D_FF), D_MODEL ** -0.5),
        "w_ff2": nrm(ks[19], (DEPTH, D_FF, D_MODEL), D_FF ** -0.5),
    }


def reference(x_prompt, x_sample, cache_k, cache_v, state_conv, page_table,
              norm_mix, w_in, q_norm, k_norm, lambda_q1, lambda_k1, lambda_q2, lambda_k2,
              subln, conv_w, w_out, norm_mlp, w_ff1, w_ff2):
    xp, xs = x_prompt, x_sample
    kp_rows, vp_rows, cp_rows, ks_rows, vs_rows, cs_rows = [], [], [], [], [], []
    for l in range(DEPTH):
        lam_init = lambda_init_fn(l)
        lam = (jnp.exp(jnp.sum(lambda_q1[l].astype(jnp.float32) * lambda_k1[l].astype(jnp.float32)))
               - jnp.exp(jnp.sum(lambda_q2[l].astype(jnp.float32) * lambda_k2[l].astype(jnp.float32)))
               + lam_init)

        q, k, v, bg, cg, xc = mixer_inputs(xp, norm_mix[l], w_in[l], q_norm[l], k_norm[l])
        a = attn_output(prompt_attention(q, k, v, lam), subln[l], lam_init, xp.dtype)
        zero_buf = jnp.zeros((xp.shape[0], CONV_K - 1, CONV_WIDTH), xp.dtype)
        c, c_state = short_conv(bg, cg, xc, zero_buf, conv_w[l])
        xp = finish_layer(xp, a, c, w_out[l], norm_mlp[l], w_ff1[l], w_ff2[l])
        kp_rows.append(k)
        vp_rows.append(v)
        cp_rows.append(c_state)

        q, k, v, bg, cg, xc = mixer_inputs(xs, norm_mix[l], w_in[l], q_norm[l], k_norm[l])
        o = sample_attention(q, k, v, cache_k[l], cache_v[l], page_table, lam)
        a = attn_output(o, subln[l], lam_init, xs.dtype)
        c, c_state = short_conv(bg, cg, xc, state_conv[l], conv_w[l])
        xs = finish_layer(xs, a, c, w_out[l], norm_mlp[l], w_ff1[l], w_ff2[l])
        ks_rows.append(k)
        vs_rows.append(v)
        cs_rows.append(c_state)

    return (xp, xs, jnp.stack(kp_rows), jnp.stack(vp_rows), jnp.stack(cp_rows),
            jnp.stack(ks_rows), jnp.stack(vs_rows), jnp.stack(cs_rows))
```

```python
import functools
import math

import jax
import jax.numpy as jnp
from jax import lax
from jax.experimental import pallas as pl
from jax.experimental.pallas import tpu as pltpu

F32 = jnp.float32
BF16 = jnp.bfloat16
EPS = 1e-6
NEG_INF = -1e30

V7X_VMEM_BYTES = 64 * 1024 * 1024
V7X_SUBLANES = 8
VMEM_LIMIT_BYTES = V7X_VMEM_BYTES - 8 * 1024 * 1024


def _lambda_init(layer):
    return 0.8 - 0.6 * math.exp(-0.3 * layer)


def _rms_rows(x, g):
    ms = jnp.mean(x * x, axis=-1, keepdims=True)
    return x * lax.rsqrt(ms + EPS) * g


def _const_spec(shape):
    zeros = (0,) * len(shape)
    return pl.BlockSpec(shape, lambda *_: zeros, pipeline_mode=pl.Buffered(1))


def _lam_value(lq1_ref, lk1_ref, lq2_ref, lk2_ref, lam_init):
    s1 = jnp.sum(lq1_ref[...] * lk1_ref[...], axis=-1, keepdims=True)
    s2 = jnp.sum(lq2_ref[...] * lk2_ref[...], axis=-1, keepdims=True)
    return jnp.exp(s1) - jnp.exp(s2) + lam_init


def _prestage_kernel(*refs, attn_w, conv_w, seq_tiles, seq_len, q_scale, sample):
    if sample:
        (x_ref, gmix_ref, win_ref, gq_ref, gk_ref, gmat_ref, cw_ref, hist_ref,
         qb_ref, k_ref, v_ref, kb_ref, vb_ref, c_ref, u_ref, ubuf) = refs
    else:
        (x_ref, gmix_ref, win_ref, gq_ref, gk_ref, gmat_ref, cw_ref,
         qb_ref, k_ref, v_ref, kb_ref, vb_ref, c_ref, cs_ref, ubuf) = refs
    tm = x_ref.shape[0]
    conv_k = cw_ref.shape[0]
    a, c = attn_w, conv_w

    h = _rms_rows(x_ref[...], gmix_ref[...]).astype(BF16)

    def proj(lo, width):
        return jnp.dot(h, win_ref[:, lo:lo + width], preferred_element_type=F32)

    gmat = gmat_ref[...]

    def group_rms(z, g):
        msq = jnp.dot((z * z).astype(BF16), gmat, preferred_element_type=F32)
        return z * lax.rsqrt(msq + EPS) * g

    q = group_rms(proj(0, a), gq_ref[...])
    qb_ref[...] = (q * q_scale).astype(BF16)
    k = group_rms(proj(a, a), gk_ref[...])
    k_ref[...] = k
    kb_ref[...] = k.astype(BF16)
    v = proj(2 * a, a)
    v_ref[...] = v
    vb_ref[...] = v.astype(BF16)
    b_gate = proj(3 * a, c)
    u = proj(3 * a + c, c) * proj(3 * a + 2 * c, c)

    halo = V7X_SUBLANES
    if sample:
        ubuf[0:halo, :] = jnp.zeros((halo, c), F32)
    else:
        first = pl.program_id(0) % seq_tiles == 0

        @pl.when(first)
        def _():
            ubuf[0:halo, :] = jnp.zeros((halo, c), F32)

        @pl.when(jnp.logical_not(first))
        def _():
            ubuf[0:halo, :] = ubuf[tm:tm + halo, :]

    ubuf[halo:halo + tm, :] = u
    if sample:
        t_pos = lax.broadcasted_iota(jnp.int32, (tm, 1), 0) % seq_len
    y = None
    for j in range(conv_k):
        shift = conv_k - 1 - j
        if shift == 0:
            tap = u
        else:
            tap = ubuf[halo - shift:halo - shift + tm, :]
            if sample:
                tap = jnp.where(t_pos >= shift, tap, hist_ref[shift - 1])
        term = cw_ref[j:j + 1, :] * tap
        y = term if y is None else y + term
    c_ref[...] = b_gate * y
    if sample:
        u_ref[...] = u
    else:
        cs_ref[0] = u[tm - (conv_k - 1):tm, :]


def _prestage(x2d, gmix, win_b, gq, gk, gmat, cw, hist, *, attn_w, conv_w, tm, seq_len, q_scale):
    t, d = x2d.shape
    sample = hist is not None
    conv_k = cw.shape[0]
    n_tiles = t // tm
    if sample:
        seq_tiles = 1
        assert n_tiles == 1 and tm % seq_len == 0
    else:
        assert seq_len % tm == 0
        seq_tiles = seq_len // tm
    row = lambda width: pl.BlockSpec((tm, width), lambda i: (i, 0))
    in_specs = [row(d), _const_spec(gmix.shape), _const_spec(win_b.shape), _const_spec(gq.shape),
                _const_spec(gk.shape), _const_spec(gmat.shape), _const_spec(cw.shape)]
    args = [x2d, gmix, win_b, gq, gk, gmat, cw]
    out_shape = [jax.ShapeDtypeStruct((t, attn_w), BF16), jax.ShapeDtypeStruct((t, attn_w), F32),
                 jax.ShapeDtypeStruct((t, attn_w), F32), jax.ShapeDtypeStruct((t, attn_w), BF16),
                 jax.ShapeDtypeStruct((t, attn_w), BF16), jax.ShapeDtypeStruct((t, conv_w), F32)]
    out_specs = [row(attn_w)] * 5 + [row(conv_w)]
    if sample:
        in_specs.append(_const_spec(hist.shape))
        args.append(hist)
        out_shape.append(jax.ShapeDtypeStruct((t, conv_w), F32))
        out_specs.append(row(conv_w))
    else:
        n_seq = t // seq_len
        out_shape.append(jax.ShapeDtypeStruct((n_seq, conv_k - 1, conv_w), F32))
        out_specs.append(pl.BlockSpec((1, conv_k - 1, conv_w), lambda i: (i // seq_tiles, 0, 0)))
    kern = functools.partial(_prestage_kernel, attn_w=attn_w, conv_w=conv_w, seq_tiles=seq_tiles,
                             seq_len=seq_len, q_scale=q_scale, sample=sample)
    return pl.pallas_call(
        kern,
        grid=(n_tiles,),
        in_specs=in_specs,
        out_specs=out_specs,
        out_shape=out_shape,
        scratch_shapes=[pltpu.VMEM((tm + 2 * V7X_SUBLANES, conv_w), F32)],
        compiler_params=pltpu.CompilerParams(dimension_semantics=("arbitrary",),
                                             vmem_limit_bytes=VMEM_LIMIT_BYTES),
        name="prestage_sample" if sample else "prestage_prompt",
    )(*args)


def _prompt_attn_kernel(q_ref, k_ref, v_ref, lq1_ref, lk1_ref, lq2_ref, lk2_ref, sub_ref, o_ref,
                        *, tq, dqk, lam_init):
    i = pl.program_id(2)
    dv = q_ref.shape[-1]
    q = q_ref[0]
    lane = lax.broadcasted_iota(jnp.int32, (tq, dv), 1)
    zero = jnp.zeros_like(q)
    qs = jnp.concatenate([jnp.where(lane < dqk, q, zero), jnp.where(lane >= dqk, q, zero)], axis=0)

    def update(j, carry, masked):
        m, l, acc = carry
        start = pl.multiple_of(j * tq, tq)
        kj = k_ref[0, pl.ds(start, tq), :]
        vj = v_ref[0, pl.ds(start, tq), :]
        s = lax.dot_general(qs, kj, (((1,), (1,)), ((), ())), preferred_element_type=F32)
        if masked:
            r = lax.broadcasted_iota(jnp.int32, s.shape, 0) % tq
            col = lax.broadcasted_iota(jnp.int32, s.shape, 1)
            s = jnp.where(r >= col, s, NEG_INF)
        m_new = jnp.maximum(m, jnp.max(s, axis=-1, keepdims=True))
        p = jnp.exp(s - m_new)
        alpha = jnp.exp(m - m_new)
        l = alpha * l + jnp.sum(p, axis=-1, keepdims=True)
        acc = alpha * acc + jnp.dot(p.astype(BF16), vj, preferred_element_type=F32)
        return m_new, l, acc

    init = (jnp.full((2 * tq, 1), NEG_INF, F32), jnp.zeros((2 * tq, 1), F32), jnp.zeros((2 * tq, dv), F32))
    carry = lax.fori_loop(0, i, lambda j, c: update(j, c, False), init)
    m, l, acc = update(i, carry, True)

    lam = _lam_value(lq1_ref, lk1_ref, lq2_ref, lk2_ref, lam_init)
    o = acc[:tq] / l[:tq] - lam * (acc[tq:] / l[tq:])
    o_ref[0] = _rms_rows(o, sub_ref[...]) * (1.0 - lam_init)


def _prompt_attention(qb, kb, vb, lams, subln, *, n_heads, dqk, tq, lam_init):
    b, s, a = qb.shape
    dv = a // n_heads
    kern = functools.partial(_prompt_attn_kernel, tq=tq, dqk=dqk, lam_init=lam_init)
    q_spec = pl.BlockSpec((1, tq, dv), lambda bi, h, i: (bi, i, h))
    kv_spec = pl.BlockSpec((1, s, dv), lambda bi, h, i: (bi, 0, h))
    small = [_const_spec(x.shape) for x in lams] + [_const_spec(subln.shape)]
    return pl.pallas_call(
        kern,
        grid=(b, n_heads, s // tq),
        in_specs=[q_spec, kv_spec, kv_spec] + small,
        out_specs=pl.BlockSpec((1, tq, dv), lambda bi, h, i: (bi, i, h)),
        out_shape=jax.ShapeDtypeStruct((b, s, a), F32),
        compiler_params=pltpu.CompilerParams(dimension_semantics=("arbitrary",) * 3,
                                             vmem_limit_bytes=VMEM_LIMIT_BYTES),
        name="prompt_attn",
    )(qb, kb, vb, *lams, subln)


def _k_page_copy(ckt_hbm, kbuf, sem, page_idx, slot, p, page):
    return pltpu.make_async_copy(ckt_hbm.at[page_idx], kbuf.at[slot, :, pl.ds(p * page, page)], sem.at[slot])


def _v_page_copy(cv_hbm, vbuf, sem, page_idx, slot, p, rows):
    return pltpu.make_async_copy(cv_hbm.at[page_idx], vbuf.at[slot, pl.ds(p * rows, rows), :], sem.at[slot])


def _sample_attn_kernel(pt_ref, q_ref, kn_ref, vn_ref, lq1_ref, lk1_ref, lq2_ref, lk2_ref, sub_ref,
                        ckt_hbm, cv_hbm, o_ref, kbuf, vbuf, ksem, vsem,
                        *, n_heads, dqk, page, ch, n_chunks, t_new, lam_init):
    b = pl.program_id(0)
    nb = pl.num_programs(0)
    dv = q_ref.shape[-1] // n_heads
    rows2 = 2 * t_new
    chunk_tokens = ch * page

    def start_chunk(bb, cc, slot):
        for p in range(ch):
            pg = pt_ref[bb, cc * ch + p]
            _k_page_copy(ckt_hbm, kbuf, ksem, pg, slot, p, page).start()
            _v_page_copy(cv_hbm, vbuf, vsem, pg, slot, p, page * n_heads).start()

    def wait_chunk(slot):
        for p in range(ch):
            _k_page_copy(ckt_hbm, kbuf, ksem, 0, slot, p, page).wait()
            _v_page_copy(cv_hbm, vbuf, vsem, 0, slot, p, page * n_heads).wait()

    @pl.when(b == 0)
    def _():
        start_chunk(0, 0, 0)

    lam = _lam_value(lq1_ref, lk1_ref, lq2_ref, lk2_ref, lam_init)
    q2 = q_ref[0].astype(F32)
    kn = kn_ref[0].astype(F32)
    vn = vn_ref[0]
    lane = lax.broadcasted_iota(jnp.int32, (rows2, dv), 1)
    rowi = lax.broadcasted_iota(jnp.int32, (rows2, dv), 0)
    lane_lo = jnp.where(rowi < t_new, 0, dqk)
    comp_mask = jnp.logical_and(lane >= lane_lo, lane < lane_lo + dqk)
    t_row = lax.broadcasted_iota(jnp.int32, (rows2, 1), 0) % t_new

    qs, state = [], []
    for h in range(n_heads):
        qf = jnp.where(comp_mask, q2[:, h * dv:(h + 1) * dv], 0.0)
        qs.append(qf)
        s_new = [jnp.sum(qf * kn[j:j + 1, h * dv:(h + 1) * dv], axis=-1, keepdims=True)
                 for j in range(t_new)]
        valid = [t_row >= j for j in range(t_new)]
        m = s_new[0]
        for j in range(1, t_new):
            m = jnp.maximum(m, jnp.where(valid[j], s_new[j], NEG_INF))
        l = jnp.zeros((rows2, 1), F32)
        acc = jnp.zeros((rows2, dv), F32)
        for j in range(t_new):
            pj = jnp.where(valid[j], jnp.exp(s_new[j] - m), 0.0)
            l = l + pj
            acc = acc + pj * vn[j:j + 1, h * dv:(h + 1) * dv]
        state.append((m, l, acc))

    for cc in range(n_chunks):
        slot = cc % 2
        if cc + 1 < n_chunks:
            start_chunk(b, cc + 1, 1 - slot)
        else:
            @pl.when(b + 1 < nb)
            def _():
                start_chunk(b + 1, 0, 1 - slot)
        wait_chunk(slot)
        for h in range(n_heads):
            m, l, acc = state[h]
            kt = kbuf[slot, h * dv:(h + 1) * dv, :]
            vh = vbuf[slot, pl.ds(h, chunk_tokens, stride=n_heads), :]
            s = jnp.dot(qs[h], kt, preferred_element_type=F32)
            m_new = jnp.maximum(m, jnp.max(s, axis=-1, keepdims=True))
            p = jnp.exp(s - m_new)
            alpha = jnp.exp(m - m_new)
            l = alpha * l + jnp.sum(p, axis=-1, keepdims=True)
            acc = alpha * acc + jnp.dot(p, vh, preferred_element_type=F32)
            state[h] = (m_new, l, acc)

    for h in range(n_heads):
        m, l, acc = state[h]
        o = acc[:t_new] / l[:t_new] - lam * (acc[t_new:] / l[t_new:])
        o_ref[0, :, h * dv:(h + 1) * dv] = _rms_rows(o, sub_ref[...]) * (1.0 - lam_init)


def _sample_attention(page_table, q2, kn, vn, lams, subln, cache_kt, cache_v,
                      *, n_heads, dqk, ch, lam_init):
    nb, rows2, a = q2.shape
    t_new = rows2 // 2
    n_pool, _, page = cache_kt.shape
    dv = cache_v.shape[-1]
    n_pages = page_table.shape[1]
    assert n_pages % ch == 0 and (n_pages // ch) % 2 == 0
    n_chunks = n_pages // ch
    kern = functools.partial(_sample_attn_kernel, n_heads=n_heads, dqk=dqk, page=page, ch=ch,
                             n_chunks=n_chunks, t_new=t_new, lam_init=lam_init)
    per_b = lambda r: pl.BlockSpec((1, r, a), lambda bi, pt: (bi, 0, 0))
    const = lambda x: pl.BlockSpec(x.shape, lambda bi, pt: (0,) * x.ndim)
    grid_spec = pltpu.PrefetchScalarGridSpec(
        num_scalar_prefetch=1,
        grid=(nb,),
        in_specs=[per_b(rows2), per_b(t_new), per_b(t_new)] + [const(x) for x in lams] + [const(subln)]
        + [pl.BlockSpec(memory_space=pl.ANY), pl.BlockSpec(memory_space=pl.ANY)],
        out_specs=per_b(t_new),
        scratch_shapes=[pltpu.VMEM((2, a, ch * page), F32), pltpu.VMEM((2, ch * page * n_heads, dv), F32),
                        pltpu.SemaphoreType.DMA((2,)), pltpu.SemaphoreType.DMA((2,))],
    )
    return pl.pallas_call(
        kern,
        grid_spec=grid_spec,
        out_shape=jax.ShapeDtypeStruct((nb, t_new, a), F32),
        compiler_params=pltpu.CompilerParams(dimension_semantics=("arbitrary",),
                                             vmem_limit_bytes=VMEM_LIMIT_BYTES),
        name="sample_attn",
    )(page_table, q2, kn, vn, *lams, subln, cache_kt, cache_v)


def _finish_kernel(x_ref, a_ref, c_ref, wout_ref, gmlp_ref, w1_ref, w2_ref, y_ref, *, ff_chunk):
    a_w = a_ref.shape[1]
    mix = jnp.dot(a_ref[...].astype(BF16), wout_ref[0:a_w, :], preferred_element_type=F32)
    mix = mix + jnp.dot(c_ref[...].astype(BF16), wout_ref[a_w:, :], preferred_element_type=F32)
    x1 = x_ref[...] + mix
    h = _rms_rows(x1, gmlp_ref[...]).astype(BF16)
    acc = x1
    d_ff = w1_ref.shape[1]
    for lo in range(0, d_ff, ff_chunk):
        z = jnp.dot(h, w1_ref[:, lo:lo + ff_chunk], preferred_element_type=F32)
        z = jnp.maximum(z, 0.0)
        z = (z * z).astype(BF16)
        acc = acc + jnp.dot(z, w2_ref[lo:lo + ff_chunk, :], preferred_element_type=F32)
    y_ref[...] = acc


def _finish(x2d, a2d, c2d, wout_b, gmlp, w1_b, w2_b, *, tm, ff_chunk):
    t, d = x2d.shape
    row = lambda width: pl.BlockSpec((tm, width), lambda i: (i, 0))
    return pl.pallas_call(
        functools.partial(_finish_kernel, ff_chunk=ff_chunk),
        grid=(t // tm,),
        in_specs=[row(d), row(a2d.shape[1]), row(c2d.shape[1]), _const_spec(wout_b.shape),
                  _const_spec(gmlp.shape), _const_spec(w1_b.shape), _const_spec(w2_b.shape)],
        out_specs=row(d),
        out_shape=jax.ShapeDtypeStruct((t, d), F32),
        compiler_params=pltpu.CompilerParams(dimension_semantics=("arbitrary",),
                                             vmem_limit_bytes=VMEM_LIMIT_BYTES),
        name="finish",
    )(x2d, a2d, c2d, wout_b, gmlp, w1_b, w2_b)


def kernel(x_prompt, x_sample, cache_k, cache_v, state_conv, page_table, norm_mix, w_in, q_norm, k_norm,
           lambda_q1, lambda_k1, lambda_q2, lambda_k2, subln, conv_w, w_out, norm_mlp, w_ff1, w_ff2):
    depth, n_pool, page, n_heads, _, dqk = cache_k.shape
    dv = cache_v.shape[-1]
    attn_w = n_heads * dv
    conv_width = state_conv.shape[-1]
    conv_k = conv_w.shape[1]
    batch, seq, d_model = x_prompt.shape
    dec_batch, dec_seq, _ = x_sample.shape
    q_scale = dqk ** -0.5
    n_groups = attn_w // dqk

    gmat = (jnp.kron(jnp.eye(n_groups, dtype=F32), jnp.ones((dqk, dqk), F32)) / dqk).astype(BF16)

    xp = x_prompt.reshape(batch * seq, d_model)
    xs = x_sample.reshape(dec_batch * dec_seq, d_model)
    outs = {name: [] for name in ("kp", "vp", "cp", "ks", "vs", "cs")}
    for l in range(depth):
        lam_init = _lambda_init(l)
        win_b = w_in[l].astype(BF16)
        wout_b = w_out[l].astype(BF16)
        w1_b = w_ff1[l].astype(BF16)
        w2_b = w_ff2[l].astype(BF16)
        gmix = norm_mix[l][None]
        gmlp = norm_mlp[l][None]
        gq = jnp.tile(q_norm[l], n_groups)[None]
        gk = jnp.tile(k_norm[l], n_groups)[None]
        lams = [lambda_q1[l][None], lambda_k1[l][None], lambda_q2[l][None], lambda_k2[l][None]]
        sub = subln[l][None]
        cw = conv_w[l]
        pre = functools.partial(_prestage, attn_w=attn_w, conv_w=conv_width, q_scale=q_scale)
        fin = functools.partial(_finish, ff_chunk=1024)

        qb, k, v, kb, vb, c, c_state = pre(xp, gmix, win_b, gq, gk, gmat, cw, None, tm=512, seq_len=seq)
        a = _prompt_attention(qb.reshape(batch, seq, attn_w), kb.reshape(batch, seq, attn_w),
                              vb.reshape(batch, seq, attn_w), lams, sub,
                              n_heads=n_heads, dqk=dqk, tq=256, lam_init=lam_init)
        xp = fin(xp, a.reshape(batch * seq, attn_w), c, wout_b, gmlp, w1_b, w2_b, tm=256)
        outs["kp"].append(k.reshape(batch, seq, n_heads, 2, dqk))
        outs["vp"].append(v.reshape(batch, seq, n_heads, dv))
        outs["cp"].append(c_state)

        st = state_conv[l]
        hist = jnp.stack([
            jnp.concatenate([st[:, conv_k - 1 - s:, :],
                             jnp.zeros((dec_batch, dec_seq - s, conv_width), F32)], axis=1)
            .reshape(dec_batch * dec_seq, conv_width)
            for s in range(1, conv_k)])
        ts = dec_batch * dec_seq
        qb, k, v, kb, vb, c, u = pre(xs, gmix, win_b, gq, gk, gmat, cw, hist, tm=ts, seq_len=dec_seq)
        q2 = jnp.tile(qb.reshape(dec_batch, dec_seq, attn_w), (1, 2, 1))
        a = _sample_attention(page_table, q2, kb.reshape(dec_batch, dec_seq, attn_w),
                              v.reshape(dec_batch, dec_seq, attn_w), lams, sub,
                              jnp.transpose(cache_k[l], (0, 2, 3, 4, 1)).reshape(n_pool, attn_w, page),
                              cache_v[l].reshape(n_pool, page * n_heads, dv),
                              n_heads=n_heads, dqk=dqk, ch=16, lam_init=lam_init)
        xs = fin(xs, a.reshape(ts, attn_w), c, wout_b, gmlp, w1_b, w2_b, tm=256)
        outs["ks"].append(k.reshape(dec_batch, dec_seq, n_heads, 2, dqk))
        outs["vs"].append(v.reshape(dec_batch, dec_seq, n_heads, dv))
        outs["cs"].append(u.reshape(dec_batch, dec_seq, conv_width)[:, dec_seq - (conv_k - 1):, :])

    return (xp.reshape(batch, seq, d_model), xs.reshape(dec_batch, dec_seq, d_model),
            jnp.stack(outs["kp"]), jnp.stack(outs["vp"]), jnp.stack(outs["cp"]),
            jnp.stack(outs["ks"]), jnp.stack(outs["vs"]), jnp.stack(outs["cs"]))
```

```python
import functools
import math

import jax
import jax.numpy as jnp
from jax import lax
from jax.experimental import pallas as pl
from jax.experimental.pallas import tpu as pltpu

F32 = jnp.float32
BF16 = jnp.bfloat16
EPS = 1e-6
NEG_INF = -1e30

V7X_VMEM_BYTES = 64 * 1024 * 1024
V7X_SUBLANES = 8
VMEM_LIMIT_BYTES = V7X_VMEM_BYTES - 8 * 1024 * 1024


def _lambda_init(layer):
    return 0.8 - 0.6 * math.exp(-0.3 * layer)


def _rms_rows(x, g):
    ms = jnp.mean(x * x, axis=-1, keepdims=True)
    return x * lax.rsqrt(ms + EPS) * g


def _const_spec(shape):
    zeros = (0,) * len(shape)
    return pl.BlockSpec(shape, lambda *_: zeros, pipeline_mode=pl.Buffered(1))


def _lam_value(lq1_ref, lk1_ref, lq2_ref, lk2_ref, lam_init):
    s1 = jnp.sum(lq1_ref[...] * lk1_ref[...], axis=-1, keepdims=True)
    s2 = jnp.sum(lq2_ref[...] * lk2_ref[...], axis=-1, keepdims=True)
    return jnp.exp(s1) - jnp.exp(s2) + lam_init


def _prestage_kernel(*refs, attn_w, conv_w, n_heads, seq_tiles, seq_len, q_scale, sample):
    if sample:
        (x_ref, gmix_ref, win_ref, gq_ref, gk_ref, gmat_ref, cw_ref, hist_ref,
         qb_ref, k_ref, v_ref, kb_ref, c_ref, u_ref, ubuf) = refs
    else:
        (x_ref, gmix_ref, win_ref, gq_ref, wkt_ref, gkt_ref, gmat_ref, cw_ref,
         qb_ref, kt_ref, ktb_ref, vil_ref, vb_ref, c_ref, cs_ref, ubuf) = refs
    tm = x_ref.shape[0]
    conv_k = cw_ref.shape[0]
    a, c = attn_w, conv_w
    dv = a // n_heads

    h = _rms_rows(x_ref[...], gmix_ref[...]).astype(BF16)

    def proj(lo, width):
        return jnp.dot(h, win_ref[:, lo:lo + width], preferred_element_type=F32)

    gmat = gmat_ref[...]

    def group_rms(z, g):
        msq = jnp.dot((z * z).astype(BF16), gmat, preferred_element_type=F32)
        return z * lax.rsqrt(msq + EPS) * g

    q = group_rms(proj(0, a), gq_ref[...])
    qb_ref[...] = (q * q_scale).astype(BF16)
    v = proj(2 * a, a)
    if sample:
        k = group_rms(proj(a, a), gk_ref[...])
        k_ref[...] = k
        kb_ref[...] = k.astype(BF16)
        v_ref[...] = v
    else:
        kt = lax.dot_general(wkt_ref[...], h, (((1,), (1,)), ((), ())), preferred_element_type=F32)
        msq = jnp.dot(gmat, (kt * kt).astype(BF16), preferred_element_type=F32)
        kt = kt * lax.rsqrt(msq + EPS) * gkt_ref[...]
        kt_ref[0] = kt
        tk = ktb_ref.shape[-1]
        for t in range(tm // tk):
            ktb_ref[0, t] = kt[:, t * tk:(t + 1) * tk].astype(BF16)
        for hh in range(n_heads):
            vil_ref[pl.ds(hh, tm, stride=n_heads), :] = v[:, hh * dv:(hh + 1) * dv]
        vb_ref[...] = v.astype(BF16)
    b_gate = proj(3 * a, c)
    u = proj(3 * a + c, c) * proj(3 * a + 2 * c, c)

    halo = V7X_SUBLANES
    if sample:
        ubuf[0:halo, :] = jnp.zeros((halo, c), F32)
    else:
        first = pl.program_id(0) % seq_tiles == 0

        @pl.when(first)
        def _():
            ubuf[0:halo, :] = jnp.zeros((halo, c), F32)

        @pl.when(jnp.logical_not(first))
        def _():
            ubuf[0:halo, :] = ubuf[tm:tm + halo, :]

    ubuf[halo:halo + tm, :] = u
    if sample:
        t_pos = lax.broadcasted_iota(jnp.int32, (tm, 1), 0) % seq_len
    y = None
    for j in range(conv_k):
        shift = conv_k - 1 - j
        if shift == 0:
            tap = u
        else:
            tap = ubuf[halo - shift:halo - shift + tm, :]
            if sample:
                tap = jnp.where(t_pos >= shift, tap, hist_ref[shift - 1])
        term = cw_ref[j:j + 1, :] * tap
        y = term if y is None else y + term
    c_ref[...] = b_gate * y
    if sample:
        u_ref[...] = u
    else:
        cs_ref[0] = u[tm - (conv_k - 1):tm, :]


def _prestage_sample(x2d, gmix, win_b, gq, gk, gmat, cw, hist, *, attn_w, conv_w, n_heads, seq_len, q_scale):
    t, d = x2d.shape
    assert t % seq_len == 0
    args = [x2d, gmix, win_b, gq, gk, gmat, cw, hist]
    out_shape = [jax.ShapeDtypeStruct((t, attn_w), BF16), jax.ShapeDtypeStruct((t, attn_w), F32),
                 jax.ShapeDtypeStruct((t, attn_w), F32), jax.ShapeDtypeStruct((t, attn_w), BF16),
                 jax.ShapeDtypeStruct((t, conv_w), F32), jax.ShapeDtypeStruct((t, conv_w), F32)]
    kern = functools.partial(_prestage_kernel, attn_w=attn_w, conv_w=conv_w, n_heads=n_heads, seq_tiles=1,
                             seq_len=seq_len, q_scale=q_scale, sample=True)
    return pl.pallas_call(
        kern,
        grid=(1,),
        in_specs=[_const_spec(x.shape) for x in args],
        out_specs=[pl.BlockSpec(s.shape, lambda i: (0, 0)) for s in out_shape],
        out_shape=out_shape,
        scratch_shapes=[pltpu.VMEM((t + 2 * V7X_SUBLANES, conv_w), F32)],
        compiler_params=pltpu.CompilerParams(dimension_semantics=("arbitrary",),
                                             vmem_limit_bytes=VMEM_LIMIT_BYTES),
        name="prestage_sample",
    )(*args)


def _prestage_prompt(x2d, gmix, win_b, gq, wkt_b, gkt, gmat, cw,
                     *, attn_w, conv_w, n_heads, tm, tk, seq_len, q_scale):
    t, d = x2d.shape
    conv_k = cw.shape[0]
    dv = attn_w // n_heads
    assert seq_len % tm == 0 and tm % tk == 0 and gkt.shape == (attn_w, tm)
    seq_tiles = seq_len // tm
    n_seq = t // seq_len
    row = lambda width: pl.BlockSpec((tm, width), lambda i: (i, 0))
    args = [x2d, gmix, win_b, gq, wkt_b, gkt, gmat, cw]
    in_specs = [row(d)] + [_const_spec(x.shape) for x in args[1:]]
    out_shape = [jax.ShapeDtypeStruct((t, attn_w), BF16),
                 jax.ShapeDtypeStruct((n_seq, attn_w, seq_len), F32),
                 jax.ShapeDtypeStruct((n_seq, seq_len // tk, attn_w, tk), BF16),
                 jax.ShapeDtypeStruct((t * n_heads, dv), F32),
                 jax.ShapeDtypeStruct((t, attn_w), BF16),
                 jax.ShapeDtypeStruct((t, conv_w), F32),
                 jax.ShapeDtypeStruct((n_seq, conv_k - 1, conv_w), F32)]
    out_specs = [row(attn_w),
                 pl.BlockSpec((1, attn_w, tm), lambda i: (i // seq_tiles, 0, i % seq_tiles)),
                 pl.BlockSpec((1, tm // tk, attn_w, tk), lambda i: (i // seq_tiles, i % seq_tiles, 0, 0)),
                 pl.BlockSpec((tm * n_heads, dv), lambda i: (i, 0)),
                 row(attn_w), row(conv_w),
                 pl.BlockSpec((1, conv_k - 1, conv_w), lambda i: (i // seq_tiles, 0, 0))]
    kern = functools.partial(_prestage_kernel, attn_w=attn_w, conv_w=conv_w, n_heads=n_heads,
                             seq_tiles=seq_tiles, seq_len=seq_len, q_scale=q_scale, sample=False)
    return pl.pallas_call(
        kern,
        grid=(t // tm,),
        in_specs=in_specs,
        out_specs=out_specs,
        out_shape=out_shape,
        scratch_shapes=[pltpu.VMEM((tm + 2 * V7X_SUBLANES, conv_w), F32)],
        compiler_params=pltpu.CompilerParams(dimension_semantics=("arbitrary",),
                                             vmem_limit_bytes=VMEM_LIMIT_BYTES),
        name="prestage_prompt",
    )(*args)


def _prompt_attn_kernel(q_ref, kt_ref, v_ref, lq1_ref, lk1_ref, lq2_ref, lk2_ref, sub_ref, o_ref,
                        *, n_heads, dqk, lam_init):
    i = pl.program_id(1)
    tq = q_ref.shape[1]
    tk = kt_ref.shape[-1]
    assert tq == tk
    dv = q_ref.shape[-1] // n_heads
    lane = lax.broadcasted_iota(jnp.int32, (tq, dv), 1)
    qs = []
    for h in range(n_heads):
        qh = q_ref[0, :, h * dv:(h + 1) * dv]
        zero = jnp.zeros_like(qh)
        qs.append(jnp.concatenate([jnp.where(lane < dqk, qh, zero), jnp.where(lane >= dqk, qh, zero)], axis=0))

    def update(j, carry, masked):
        start = pl.multiple_of(j * tk, tk)
        out = []
        for h in range(n_heads):
            m, l, acc = carry[h]
            ktj = kt_ref[0, j, h * dv:(h + 1) * dv, :]
            vj = v_ref[0, pl.ds(start, tk), h * dv:(h + 1) * dv]
            s = jnp.dot(qs[h], ktj, preferred_element_type=F32)
            if masked:
                r = lax.broadcasted_iota(jnp.int32, s.shape, 0) % tq
                col = lax.broadcasted_iota(jnp.int32, s.shape, 1)
                s = jnp.where(r >= col, s, NEG_INF)
            m_new = jnp.maximum(m, jnp.max(s, axis=-1, keepdims=True))
            p = jnp.exp(s - m_new)
            alpha = jnp.exp(m - m_new)
            l = alpha * l + jnp.sum(p, axis=-1, keepdims=True)
            acc = alpha * acc + jnp.dot(p.astype(BF16), vj, preferred_element_type=F32)
            out.append((m_new, l, acc))
        return tuple(out)

    init = tuple((jnp.full((2 * tq, 1), NEG_INF, F32), jnp.zeros((2 * tq, 1), F32),
                  jnp.zeros((2 * tq, dv), F32)) for _ in range(n_heads))
    carry = lax.fori_loop(0, i, lambda j, c: update(j, c, False), init)
    carry = update(i, carry, True)

    lam = _lam_value(lq1_ref, lk1_ref, lq2_ref, lk2_ref, lam_init)
    for h in range(n_heads):
        m, l, acc = carry[h]
        o = acc[:tq] / l[:tq] - lam * (acc[tq:] / l[tq:])
        o_ref[0, :, h * dv:(h + 1) * dv] = _rms_rows(o, sub_ref[...]) * (1.0 - lam_init)


def _prompt_attention(qb, ktb, vb, lams, subln, *, n_heads, dqk, lam_init):
    b, s, a = qb.shape
    nk, tk = ktb.shape[1], ktb.shape[3]
    kern = functools.partial(_prompt_attn_kernel, n_heads=n_heads, dqk=dqk, lam_init=lam_init)
    small = [_const_spec(x.shape) for x in lams] + [_const_spec(subln.shape)]
    return pl.pallas_call(
        kern,
        grid=(b, s // tk),
        in_specs=[pl.BlockSpec((1, tk, a), lambda bi, i: (bi, i, 0)),
                  pl.BlockSpec((1, nk, a, tk), lambda bi, i: (bi, 0, 0, 0)),
                  pl.BlockSpec((1, s, a), lambda bi, i: (bi, 0, 0))] + small,
        out_specs=pl.BlockSpec((1, tk, a), lambda bi, i: (bi, i, 0)),
        out_shape=jax.ShapeDtypeStruct((b, s, a), F32),
        compiler_params=pltpu.CompilerParams(dimension_semantics=("arbitrary",) * 2,
                                             vmem_limit_bytes=VMEM_LIMIT_BYTES),
        name="prompt_attn",
    )(qb, ktb, vb, *lams, subln)


def _k_page_copy(ckt_hbm, kbuf, sem, page_idx, slot, p, page):
    return pltpu.make_async_copy(ckt_hbm.at[page_idx], kbuf.at[slot, :, pl.ds(p * page, page)], sem.at[slot])


def _v_page_copy(cv_hbm, vbuf, sem, page_idx, slot, p, rows):
    return pltpu.make_async_copy(cv_hbm.at[page_idx], vbuf.at[slot, pl.ds(p * rows, rows), :], sem.at[slot])


def _sample_attn_kernel(pt_ref, q_ref, kn_ref, vn_ref, lq1_ref, lk1_ref, lq2_ref, lk2_ref, sub_ref,
                        ckt_hbm, cv_hbm, o_ref, kbuf, vbuf, ksem, vsem,
                        *, n_heads, dqk, page, ch, n_chunks, t_new, lam_init):
    b = pl.program_id(0)
    nb = pl.num_programs(0)
    dv = q_ref.shape[-1] // n_heads
    rows2 = 2 * t_new
    chunk_tokens = ch * page

    def start_chunk(bb, cc, slot):
        for p in range(ch):
            pg = pt_ref[bb, cc * ch + p]
            _k_page_copy(ckt_hbm, kbuf, ksem, pg, slot, p, page).start()
            _v_page_copy(cv_hbm, vbuf, vsem, pg, slot, p, page * n_heads).start()

    def wait_chunk(slot):
        for p in range(ch):
            _k_page_copy(ckt_hbm, kbuf, ksem, 0, slot, p, page).wait()
            _v_page_copy(cv_hbm, vbuf, vsem, 0, slot, p, page * n_heads).wait()

    @pl.when(b == 0)
    def _():
        start_chunk(0, 0, 0)

    lam = _lam_value(lq1_ref, lk1_ref, lq2_ref, lk2_ref, lam_init)
    q2 = q_ref[0].astype(F32)
    kn = kn_ref[0].astype(F32)
    vn = vn_ref[0]
    lane = lax.broadcasted_iota(jnp.int32, (rows2, dv), 1)
    rowi = lax.broadcasted_iota(jnp.int32, (rows2, dv), 0)
    lane_lo = jnp.where(rowi < t_new, 0, dqk)
    comp_mask = jnp.logical_and(lane >= lane_lo, lane < lane_lo + dqk)
    t_row = lax.broadcasted_iota(jnp.int32, (rows2, 1), 0) % t_new

    qs, state = [], []
    for h in range(n_heads):
        qf = jnp.where(comp_mask, q2[:, h * dv:(h + 1) * dv], 0.0)
        qs.append(qf)
        s_new = [jnp.sum(qf * kn[j:j + 1, h * dv:(h + 1) * dv], axis=-1, keepdims=True)
                 for j in range(t_new)]
        valid = [t_row >= j for j in range(t_new)]
        m = s_new[0]
        for j in range(1, t_new):
            m = jnp.maximum(m, jnp.where(valid[j], s_new[j], NEG_INF))
        l = jnp.zeros((rows2, 1), F32)
        acc = jnp.zeros((rows2, dv), F32)
        for j in range(t_new):
            pj = jnp.where(valid[j], jnp.exp(s_new[j] - m), 0.0)
            l = l + pj
            acc = acc + pj * vn[j:j + 1, h * dv:(h + 1) * dv]
        state.append((m, l, acc))

    for cc in range(n_chunks):
        slot = cc % 2
        if cc + 1 < n_chunks:
            start_chunk(b, cc + 1, 1 - slot)
        else:
            @pl.when(b + 1 < nb)
            def _():
                start_chunk(b + 1, 0, 1 - slot)
        wait_chunk(slot)
        for h in range(n_heads):
            m, l, acc = state[h]
            kt = kbuf[slot, h * dv:(h + 1) * dv, :]
            vh = vbuf[slot, pl.ds(h, chunk_tokens, stride=n_heads), :]
            s = jnp.dot(qs[h], kt, preferred_element_type=F32)
            m_new = jnp.maximum(m, jnp.max(s, axis=-1, keepdims=True))
            p = jnp.exp(s - m_new)
            alpha = jnp.exp(m - m_new)
            l = alpha * l + jnp.sum(p, axis=-1, keepdims=True)
            acc = alpha * acc + jnp.dot(p, vh, preferred_element_type=F32)
            state[h] = (m_new, l, acc)

    for h in range(n_heads):
        m, l, acc = state[h]
        o = acc[:t_new] / l[:t_new] - lam * (acc[t_new:] / l[t_new:])
        o_ref[0, :, h * dv:(h + 1) * dv] = _rms_rows(o, sub_ref[...]) * (1.0 - lam_init)


def _sample_attention(page_table, q2, kn, vn, lams, subln, cache_kt, cache_v,
                      *, n_heads, dqk, ch, lam_init):
    nb, rows2, a = q2.shape
    t_new = rows2 // 2
    n_pool, _, page = cache_kt.shape
    dv = cache_v.shape[-1]
    n_pages = page_table.shape[1]
    assert n_pages % ch == 0 and (n_pages // ch) % 2 == 0
    n_chunks = n_pages // ch
    kern = functools.partial(_sample_attn_kernel, n_heads=n_heads, dqk=dqk, page=page, ch=ch,
                             n_chunks=n_chunks, t_new=t_new, lam_init=lam_init)
    per_b = lambda r: pl.BlockSpec((1, r, a), lambda bi, pt: (bi, 0, 0))
    const = lambda x: pl.BlockSpec(x.shape, lambda bi, pt: (0,) * x.ndim)
    grid_spec = pltpu.PrefetchScalarGridSpec(
        num_scalar_prefetch=1,
        grid=(nb,),
        in_specs=[per_b(rows2), per_b(t_new), per_b(t_new)] + [const(x) for x in lams] + [const(subln)]
        + [pl.BlockSpec(memory_space=pl.ANY), pl.BlockSpec(memory_space=pl.ANY)],
        out_specs=per_b(t_new),
        scratch_shapes=[pltpu.VMEM((2, a, ch * page), F32), pltpu.VMEM((2, ch * page * n_heads, dv), F32),
                        pltpu.SemaphoreType.DMA((2,)), pltpu.SemaphoreType.DMA((2,))],
    )
    return pl.pallas_call(
        kern,
        grid_spec=grid_spec,
        out_shape=jax.ShapeDtypeStruct((nb, t_new, a), F32),
        compiler_params=pltpu.CompilerParams(dimension_semantics=("arbitrary",),
                                             vmem_limit_bytes=VMEM_LIMIT_BYTES),
        name="sample_attn",
    )(page_table, q2, kn, vn, *lams, subln, cache_kt, cache_v)


def _finish_kernel(x_ref, a_ref, c_ref, wout_ref, gmlp_ref, w1_ref, w2_ref, y_ref, *, ff_chunk):
    a_w = a_ref.shape[1]
    mix = jnp.dot(a_ref[...].astype(BF16), wout_ref[0:a_w, :], preferred_element_type=F32)
    mix = mix + jnp.dot(c_ref[...].astype(BF16), wout_ref[a_w:, :], preferred_element_type=F32)
    x1 = x_ref[...] + mix
    h = _rms_rows(x1, gmlp_ref[...]).astype(BF16)
    acc = x1
    d_ff = w1_ref.shape[1]
    for lo in range(0, d_ff, ff_chunk):
        z = jnp.dot(h, w1_ref[:, lo:lo + ff_chunk], preferred_element_type=F32)
        z = jnp.maximum(z, 0.0)
        z = (z * z).astype(BF16)
        acc = acc + jnp.dot(z, w2_ref[lo:lo + ff_chunk, :], preferred_element_type=F32)
    y_ref[...] = acc


def _finish(x2d, a2d, c2d, wout_b, gmlp, w1_b, w2_b, *, tm, ff_chunk):
    t, d = x2d.shape
    row = lambda width: pl.BlockSpec((tm, width), lambda i: (i, 0))
    return pl.pallas_call(
        functools.partial(_finish_kernel, ff_chunk=ff_chunk),
        grid=(t // tm,),
        in_specs=[row(d), row(a2d.shape[1]), row(c2d.shape[1]), _const_spec(wout_b.shape),
                  _const_spec(gmlp.shape), _const_spec(w1_b.shape), _const_spec(w2_b.shape)],
        out_specs=row(d),
        out_shape=jax.ShapeDtypeStruct((t, d), F32),
        compiler_params=pltpu.CompilerParams(dimension_semantics=("arbitrary",),
                                             vmem_limit_bytes=VMEM_LIMIT_BYTES),
        name="finish",
    )(x2d, a2d, c2d, wout_b, gmlp, w1_b, w2_b)


def kernel(x_prompt, x_sample, cache_k, cache_v, state_conv, page_table, norm_mix, w_in, q_norm, k_norm,
           lambda_q1, lambda_k1, lambda_q2, lambda_k2, subln, conv_w, w_out, norm_mlp, w_ff1, w_ff2):
    depth, n_pool, page, n_heads, _, dqk = cache_k.shape
    dv = cache_v.shape[-1]
    attn_w = n_heads * dv
    conv_width = state_conv.shape[-1]
    conv_k = conv_w.shape[1]
    batch, seq, d_model = x_prompt.shape
    dec_batch, dec_seq, _ = x_sample.shape
    q_scale = dqk ** -0.5
    n_groups = attn_w // dqk

    gmat = (jnp.kron(jnp.eye(n_groups, dtype=F32), jnp.ones((dqk, dqk), F32)) / dqk).astype(BF16)

    xp = x_prompt.reshape(batch * seq, d_model)
    xs = x_sample.reshape(dec_batch * dec_seq, d_model)
    outs = {name: [] for name in ("kp", "vp", "cp", "ks", "vs", "cs")}
    for l in range(depth):
        lam_init = _lambda_init(l)
        win_b = w_in[l].astype(BF16)
        wout_b = w_out[l].astype(BF16)
        w1_b = w_ff1[l].astype(BF16)
        w2_b = w_ff2[l].astype(BF16)
        gmix = norm_mix[l][None]
        gmlp = norm_mlp[l][None]
        gq = jnp.tile(q_norm[l], n_groups)[None]
        gk = jnp.tile(k_norm[l], n_groups)[None]
        lams = [lambda_q1[l][None], lambda_k1[l][None], lambda_q2[l][None], lambda_k2[l][None]]
        sub = subln[l][None]
        cw = conv_w[l]
        pre = dict(attn_w=attn_w, conv_w=conv_width, n_heads=n_heads, q_scale=q_scale)
        fin = functools.partial(_finish, ff_chunk=1024)

        tm_p, tk_p = 512, 256
        wkt_b = w_in[l][:, attn_w:2 * attn_w].T.astype(BF16)
        gkt = jnp.broadcast_to(gk.reshape(attn_w, 1), (attn_w, tm_p))
        qb, kt, ktb, v_il, vb, c, c_state = _prestage_prompt(
            xp, gmix, win_b, gq, wkt_b, gkt, gmat, cw, tm=tm_p, tk=tk_p, seq_len=seq, **pre)
        a = _prompt_attention(qb.reshape(batch, seq, attn_w), ktb, vb.reshape(batch, seq, attn_w), lams, sub,
                              n_heads=n_heads, dqk=dqk, lam_init=lam_init)
        xp = fin(xp, a.reshape(batch * seq, attn_w), c, wout_b, gmlp, w1_b, w2_b, tm=256)
        outs["kp"].append(jnp.transpose(kt.reshape(batch, n_heads, 2, dqk, seq), (0, 4, 1, 2, 3)))
        outs["vp"].append(v_il.reshape(batch, seq, n_heads, dv))
        outs["cp"].append(c_state)

        st = state_conv[l]
        hist = jnp.stack([
            jnp.concatenate([st[:, conv_k - 1 - s:, :],
                             jnp.zeros((dec_batch, dec_seq - s, conv_width), F32)], axis=1)
            .reshape(dec_batch * dec_seq, conv_width)
            for s in range(1, conv_k)])
        ts = dec_batch * dec_seq
        qb, k, v, kb, c, u = _prestage_sample(xs, gmix, win_b, gq, gk, gmat, cw, hist, seq_len=dec_seq, **pre)
        q2 = jnp.tile(qb.reshape(dec_batch, dec_seq, attn_w), (1, 2, 1))
        a = _sample_attention(page_table, q2, kb.reshape(dec_batch, dec_seq, attn_w),
                              v.reshape(dec_batch, dec_seq, attn_w), lams, sub,
                              jnp.transpose(cache_k[l], (0, 2, 3, 4, 1)).reshape(n_pool, attn_w, page),
                              cache_v[l].reshape(n_pool, page * n_heads, dv),
                              n_heads=n_heads, dqk=dqk, ch=16, lam_init=lam_init)
        xs = fin(xs, a.reshape(ts, attn_w), c, wout_b, gmlp, w1_b, w2_b, tm=256)
        outs["ks"].append(k.reshape(dec_batch, dec_seq, n_heads, 2, dqk))
        outs["vs"].append(v.reshape(dec_batch, dec_seq, n_heads, dv))
        outs["cs"].append(u.reshape(dec_batch, dec_seq, conv_width)[:, dec_seq - (conv_k - 1):, :])

    return (xp.reshape(batch, seq, d_model), xs.reshape(dec_batch, dec_seq, d_model),
            jnp.stack(outs["kp"]), jnp.stack(outs["vp"]), jnp.stack(outs["cp"]),
            jnp.stack(outs["ks"]), jnp.stack(outs["vs"]), jnp.stack(outs["cs"]))
```

```python
import functools
import math

import jax
import jax.numpy as jnp
from jax import lax
from jax.experimental import pallas as pl
from jax.experimental.pallas import tpu as pltpu

F32 = jnp.float32
BF16 = jnp.bfloat16
EPS = 1e-6
NEG_INF = -1e30

V7X_VMEM_BYTES = 64 * 1024 * 1024
V7X_SUBLANES = 8
VMEM_LIMIT_BYTES = V7X_VMEM_BYTES - 8 * 1024 * 1024


def _lambda_init(layer):
    return 0.8 - 0.6 * math.exp(-0.3 * layer)


def _rms_rows(x, g):
    ms = jnp.mean(x * x, axis=-1, keepdims=True)
    return x * lax.rsqrt(ms + EPS) * g


def _const_spec(shape):
    zeros = (0,) * len(shape)
    return pl.BlockSpec(shape, lambda *_: zeros, pipeline_mode=pl.Buffered(1))


def _lam_value(lq1_ref, lk1_ref, lq2_ref, lk2_ref, lam_init):
    s1 = jnp.sum(lq1_ref[...] * lk1_ref[...], axis=-1, keepdims=True)
    s2 = jnp.sum(lq2_ref[...] * lk2_ref[...], axis=-1, keepdims=True)
    return jnp.exp(s1) - jnp.exp(s2) + lam_init


def _prestage_kernel(*refs, attn_w, conv_w, n_heads, seq_tiles, seq_len, q_scale, sample):
    if sample:
        (x_ref, gmix_ref, win_ref, gq_ref, gk_ref, gmat_ref, cw_ref, hist_ref,
         qb_ref, k_ref, v_ref, kb_ref, c_ref, u_ref, ubuf) = refs
    else:
        (x_ref, gmix_ref, win_ref, gq_ref, wkt_ref, gkt_ref, gmat_ref, cw_ref,
         qb_ref, kt_ref, ktb_ref, vil_ref, vb_ref, c_ref, cs_ref, ubuf) = refs
    tm = x_ref.shape[0]
    conv_k = cw_ref.shape[0]
    a, c = attn_w, conv_w
    dv = a // n_heads

    h = _rms_rows(x_ref[...], gmix_ref[...]).astype(BF16)

    def proj(lo, width):
        return jnp.dot(h, win_ref[:, lo:lo + width], preferred_element_type=F32)

    gmat = gmat_ref[...]

    def group_rms(z, g):
        msq = jnp.dot((z * z).astype(BF16), gmat, preferred_element_type=F32)
        return z * lax.rsqrt(msq + EPS) * g

    q = group_rms(proj(0, a), gq_ref[...])
    qb_ref[...] = (q * q_scale).astype(BF16)
    v = proj(2 * a, a)
    if sample:
        k = group_rms(proj(a, a), gk_ref[...])
        k_ref[...] = k
        kb_ref[...] = k.astype(BF16)
        v_ref[...] = v
    else:
        kt = lax.dot_general(wkt_ref[...], h, (((1,), (1,)), ((), ())), preferred_element_type=F32)
        msq = jnp.dot(gmat, (kt * kt).astype(BF16), preferred_element_type=F32)
        kt = kt * lax.rsqrt(msq + EPS) * gkt_ref[...]
        kt_ref[0] = kt
        tk = ktb_ref.shape[-1]
        for t in range(tm // tk):
            ktb_ref[0, t] = kt[:, t * tk:(t + 1) * tk].astype(BF16)
        for hh in range(n_heads):
            vil_ref[pl.ds(hh, tm, stride=n_heads), :] = v[:, hh * dv:(hh + 1) * dv]
        vb_ref[...] = v.astype(BF16)
    b_gate = proj(3 * a, c)
    u = proj(3 * a + c, c) * proj(3 * a + 2 * c, c)

    halo = V7X_SUBLANES
    if sample:
        ubuf[0:halo, :] = jnp.zeros((halo, c), F32)
    else:
        first = pl.program_id(0) % seq_tiles == 0

        @pl.when(first)
        def _():
            ubuf[0:halo, :] = jnp.zeros((halo, c), F32)

        @pl.when(jnp.logical_not(first))
        def _():
            ubuf[0:halo, :] = ubuf[tm:tm + halo, :]

    ubuf[halo:halo + tm, :] = u
    if sample:
        t_pos = lax.broadcasted_iota(jnp.int32, (tm, 1), 0) % seq_len
    y = None
    for j in range(conv_k):
        shift = conv_k - 1 - j
        if shift == 0:
            tap = u
        else:
            tap = ubuf[halo - shift:halo - shift + tm, :]
            if sample:
                tap = jnp.where(t_pos >= shift, tap, hist_ref[shift - 1])
        term = cw_ref[j:j + 1, :] * tap
        y = term if y is None else y + term
    c_ref[...] = b_gate * y
    if sample:
        u_ref[...] = u
    else:
        cs_ref[0] = u[tm - (conv_k - 1):tm, :]


def _prestage_sample(x2d, gmix, win_b, gq, gk, gmat, cw, hist, *, attn_w, conv_w, n_heads, seq_len, q_scale):
    t, d = x2d.shape
    assert t % seq_len == 0
    args = [x2d, gmix, win_b, gq, gk, gmat, cw, hist]
    out_shape = [jax.ShapeDtypeStruct((t, attn_w), BF16), jax.ShapeDtypeStruct((t, attn_w), F32),
                 jax.ShapeDtypeStruct((t, attn_w), F32), jax.ShapeDtypeStruct((t, attn_w), BF16),
                 jax.ShapeDtypeStruct((t, conv_w), F32), jax.ShapeDtypeStruct((t, conv_w), F32)]
    kern = functools.partial(_prestage_kernel, attn_w=attn_w, conv_w=conv_w, n_heads=n_heads, seq_tiles=1,
                             seq_len=seq_len, q_scale=q_scale, sample=True)
    return pl.pallas_call(
        kern,
        grid=(1,),
        in_specs=[_const_spec(x.shape) for x in args],
        out_specs=[pl.BlockSpec(s.shape, lambda i: (0, 0)) for s in out_shape],
        out_shape=out_shape,
        scratch_shapes=[pltpu.VMEM((t + 2 * V7X_SUBLANES, conv_w), F32)],
        compiler_params=pltpu.CompilerParams(dimension_semantics=("arbitrary",),
                                             vmem_limit_bytes=VMEM_LIMIT_BYTES),
        name="prestage_sample",
    )(*args)


def _prestage_prompt(x2d, gmix, win_b, gq, wkt_b, gkt, gmat, cw,
                     *, attn_w, conv_w, n_heads, tm, tk, seq_len, q_scale):
    t, d = x2d.shape
    conv_k = cw.shape[0]
    dv = attn_w // n_heads
    assert seq_len % tm == 0 and tm % tk == 0 and gkt.shape == (attn_w, tm)
    seq_tiles = seq_len // tm
    n_seq = t // seq_len
    row = lambda width: pl.BlockSpec((tm, width), lambda i: (i, 0))
    args = [x2d, gmix, win_b, gq, wkt_b, gkt, gmat, cw]
    in_specs = [row(d)] + [_const_spec(x.shape) for x in args[1:]]
    out_shape = [jax.ShapeDtypeStruct((t, attn_w), BF16),
                 jax.ShapeDtypeStruct((n_seq, attn_w, seq_len), F32),
                 jax.ShapeDtypeStruct((n_seq, seq_len // tk, attn_w, tk), BF16),
                 jax.ShapeDtypeStruct((t * n_heads, dv), F32),
                 jax.ShapeDtypeStruct((t, attn_w), BF16),
                 jax.ShapeDtypeStruct((t, conv_w), F32),
                 jax.ShapeDtypeStruct((n_seq, conv_k - 1, conv_w), F32)]
    out_specs = [row(attn_w),
                 pl.BlockSpec((1, attn_w, tm), lambda i: (i // seq_tiles, 0, i % seq_tiles)),
                 pl.BlockSpec((1, tm // tk, attn_w, tk), lambda i: (i // seq_tiles, i % seq_tiles, 0, 0)),
                 pl.BlockSpec((tm * n_heads, dv), lambda i: (i, 0)),
                 row(attn_w), row(conv_w),
                 pl.BlockSpec((1, conv_k - 1, conv_w), lambda i: (i // seq_tiles, 0, 0))]
    kern = functools.partial(_prestage_kernel, attn_w=attn_w, conv_w=conv_w, n_heads=n_heads,
                             seq_tiles=seq_tiles, seq_len=seq_len, q_scale=q_scale, sample=False)
    return pl.pallas_call(
        kern,
        grid=(t // tm,),
        in_specs=in_specs,
        out_specs=out_specs,
        out_shape=out_shape,
        scratch_shapes=[pltpu.VMEM((tm + 2 * V7X_SUBLANES, conv_w), F32)],
        compiler_params=pltpu.CompilerParams(dimension_semantics=("arbitrary",),
                                             vmem_limit_bytes=VMEM_LIMIT_BYTES),
        name="prestage_prompt",
    )(*args)


def _prompt_attn_kernel(q_ref, kt_ref, v_ref, lq1_ref, lk1_ref, lq2_ref, lk2_ref, sub_ref, o_ref,
                        *, n_heads, dqk, lam_init):
    i = pl.program_id(1)
    tq = q_ref.shape[1]
    tk = kt_ref.shape[-1]
    assert tq == tk
    dv = q_ref.shape[-1] // n_heads
    lane = lax.broadcasted_iota(jnp.int32, (tq, dv), 1)
    qs = []
    for h in range(n_heads):
        qh = q_ref[0, :, h * dv:(h + 1) * dv]
        zero = jnp.zeros_like(qh)
        qs.append(jnp.concatenate([jnp.where(lane < dqk, qh, zero), jnp.where(lane >= dqk, qh, zero)], axis=0))

    def update(j, carry, masked):
        start = pl.multiple_of(j * tk, tk)
        out = []
        for h in range(n_heads):
            m, l, acc = carry[h]
            ktj = kt_ref[0, j, h * dv:(h + 1) * dv, :]
            vj = v_ref[0, pl.ds(start, tk), h * dv:(h + 1) * dv]
            s = jnp.dot(qs[h], ktj, preferred_element_type=F32)
            if masked:
                r = lax.broadcasted_iota(jnp.int32, s.shape, 0) % tq
                col = lax.broadcasted_iota(jnp.int32, s.shape, 1)
                s = jnp.where(r >= col, s, NEG_INF)
            m_new = jnp.maximum(m, jnp.max(s, axis=-1, keepdims=True))
            p = jnp.exp(s - m_new)
            alpha = jnp.exp(m - m_new)
            l = alpha * l + jnp.sum(p, axis=-1, keepdims=True)
            acc = alpha * acc + jnp.dot(p.astype(BF16), vj, preferred_element_type=F32)
            out.append((m_new, l, acc))
        return tuple(out)

    init = tuple((jnp.full((2 * tq, 1), NEG_INF, F32), jnp.zeros((2 * tq, 1), F32),
                  jnp.zeros((2 * tq, dv), F32)) for _ in range(n_heads))
    carry = lax.fori_loop(0, i, lambda j, c: update(j, c, False), init)
    carry = update(i, carry, True)

    lam = _lam_value(lq1_ref, lk1_ref, lq2_ref, lk2_ref, lam_init)
    for h in range(n_heads):
        m, l, acc = carry[h]
        o = acc[:tq] / l[:tq] - lam * (acc[tq:] / l[tq:])
        o_ref[0, :, h * dv:(h + 1) * dv] = _rms_rows(o, sub_ref[...]) * (1.0 - lam_init)


def _prompt_attention(qb, ktb, vb, lams, subln, *, n_heads, dqk, lam_init):
    b, s, a = qb.shape
    nk, tk = ktb.shape[1], ktb.shape[3]
    kern = functools.partial(_prompt_attn_kernel, n_heads=n_heads, dqk=dqk, lam_init=lam_init)
    small = [_const_spec(x.shape) for x in lams] + [_const_spec(subln.shape)]
    return pl.pallas_call(
        kern,
        grid=(b, s // tk),
        in_specs=[pl.BlockSpec((1, tk, a), lambda bi, i: (bi, i, 0)),
                  pl.BlockSpec((1, nk, a, tk), lambda bi, i: (bi, 0, 0, 0)),
                  pl.BlockSpec((1, s, a), lambda bi, i: (bi, 0, 0))] + small,
        out_specs=pl.BlockSpec((1, tk, a), lambda bi, i: (bi, i, 0)),
        out_shape=jax.ShapeDtypeStruct((b, s, a), F32),
        compiler_params=pltpu.CompilerParams(dimension_semantics=("arbitrary",) * 2,
                                             vmem_limit_bytes=VMEM_LIMIT_BYTES),
        name="prompt_attn",
    )(qb, ktb, vb, *lams, subln)


def _k_page_copy(ckt_hbm, kbuf, sem, page_idx, slot, p, page):
    return pltpu.make_async_copy(ckt_hbm.at[page_idx], kbuf.at[slot, :, pl.ds(p * page, page)], sem.at[slot])


def _v_page_copy(cv_hbm, vbuf, sem, page_idx, slot, p, rows):
    return pltpu.make_async_copy(cv_hbm.at[page_idx], vbuf.at[slot, pl.ds(p * rows, rows), :], sem.at[slot])


class _SampleStream:
    def __init__(self, pt_ref, ckt_hbm, cv_hbm, kbuf, vbuf, ksem, vsem, *, n_seqs, n_heads, dqk, dv, page, ch,
                 n_chunks, t_new):
        assert n_chunks % 2 == 0 and n_heads % 2 == 0
        self.pt_ref, self.ckt_hbm, self.cv_hbm = pt_ref, ckt_hbm, cv_hbm
        self.kbuf, self.vbuf, self.ksem, self.vsem = kbuf, vbuf, ksem, vsem
        self.n_seqs, self.n_heads, self.dqk, self.dv = n_seqs, n_heads, dqk, dv
        self.page, self.ch, self.n_chunks, self.t_new = page, ch, n_chunks, t_new

    def start(self, seq, chunk, slot):
        for p in range(self.ch):
            pg = self.pt_ref[seq, chunk * self.ch + p]
            _k_page_copy(self.ckt_hbm, self.kbuf, self.ksem, pg, slot, p, self.page).start()
            _v_page_copy(self.cv_hbm, self.vbuf, self.vsem, pg, slot, p, self.page * self.n_heads).start()

    def wait(self, slot):
        for p in range(self.ch):
            _k_page_copy(self.ckt_hbm, self.kbuf, self.ksem, 0, slot, p, self.page).wait()
            _v_page_copy(self.cv_hbm, self.vbuf, self.vsem, 0, slot, p, self.page * self.n_heads).wait()

    def prologue(self):
        self.start(0, 0, 0)
        self.start(0, 1, 1)

    def start_after(self, seq, chunk):
        slot = chunk % 2
        nxt = chunk + 2
        if nxt < self.n_chunks:
            self.start(seq, nxt, slot)
        else:
            @pl.when(seq + 1 < self.n_seqs)
            def _():
                self.start(seq + 1, nxt - self.n_chunks, slot)

    def init_state(self, q2, kn, vn):
        t_new, dv, dqk = self.t_new, self.dv, self.dqk
        rows2 = 2 * t_new
        lane = lax.broadcasted_iota(jnp.int32, (rows2, dv), 1)
        rowi = lax.broadcasted_iota(jnp.int32, (rows2, dv), 0)
        lane_lo = jnp.where(rowi < t_new, 0, dqk)
        comp_mask = jnp.logical_and(lane >= lane_lo, lane < lane_lo + dqk)
        t_row = lax.broadcasted_iota(jnp.int32, (rows2, 1), 0) % t_new
        zero = jnp.zeros((rows2, dv), F32)
        qbd, state = [], []
        for pair in range(self.n_heads // 2):
            per_head = []
            for h in (2 * pair, 2 * pair + 1):
                qf = jnp.where(comp_mask, q2[:, h * dv:(h + 1) * dv], 0.0)
                s_new = [jnp.sum(qf * kn[j:j + 1, h * dv:(h + 1) * dv], axis=-1, keepdims=True)
                         for j in range(t_new)]
                valid = [t_row >= j for j in range(t_new)]
                m = s_new[0]
                for j in range(1, t_new):
                    m = jnp.maximum(m, jnp.where(valid[j], s_new[j], NEG_INF))
                l = jnp.zeros((rows2, 1), F32)
                acc = zero
                for j in range(t_new):
                    pj = jnp.where(valid[j], jnp.exp(s_new[j] - m), 0.0)
                    l = l + pj
                    acc = acc + pj * vn[j:j + 1, h * dv:(h + 1) * dv]
                per_head.append((qf, m, l, acc))
            (q0, m0, l0, a0), (q1, m1, l1, a1) = per_head
            qbd.append(jnp.concatenate([jnp.concatenate([q0, zero], axis=1),
                                        jnp.concatenate([zero, q1], axis=1)], axis=0))
            state.append((jnp.concatenate([m0, m1], axis=0), jnp.concatenate([l0, l1], axis=0),
                          jnp.concatenate([jnp.concatenate([a0, zero], axis=1),
                                           jnp.concatenate([zero, a1], axis=1)], axis=0)))
        return qbd, state

    def consume(self, slot, qbd, state):
        dv = self.dv
        tokens = self.ch * self.page
        out = []
        for pair, (m, l, acc) in enumerate(state):
            h0, h1 = 2 * pair, 2 * pair + 1
            kt = self.kbuf[slot, h0 * dv:(h1 + 1) * dv, :]
            s = jnp.dot(qbd[pair], kt, preferred_element_type=F32)
            m_new = jnp.maximum(m, jnp.max(s, axis=-1, keepdims=True))
            p = jnp.exp(s - m_new)
            alpha = jnp.exp(m - m_new)
            l = alpha * l + jnp.sum(p, axis=-1, keepdims=True)
            v2 = jnp.concatenate([self.vbuf[slot, pl.ds(h0, tokens, stride=self.n_heads), :],
                                  self.vbuf[slot, pl.ds(h1, tokens, stride=self.n_heads), :]], axis=1)
            acc = alpha * acc + jnp.dot(p, v2, preferred_element_type=F32)
            out.append((m_new, l, acc))
        return out

    def finalize(self, state, lam, sub, lam_init, store):
        t_new, dv = self.t_new, self.dv
        rows2 = 2 * t_new
        for pair, (m, l, acc) in enumerate(state):
            for k in range(2):
                a_h = acc[k * rows2:(k + 1) * rows2, k * dv:(k + 1) * dv]
                l_h = l[k * rows2:(k + 1) * rows2]
                o = a_h[:t_new] / l_h[:t_new] - lam * (a_h[t_new:] / l_h[t_new:])
                store(2 * pair + k, _rms_rows(o, sub) * (1.0 - lam_init))


def _finish_parts(x_ref, a_ref, c_ref, wout_ref, gmlp_ref, w1_ref, w2_ref, y_ref, ff_chunk):
    a_w = a_ref.shape[1]
    d_ff = w1_ref.shape[1]
    st = {}

    def project():
        mix = jnp.dot(a_ref[...].astype(BF16), wout_ref[0:a_w, :], preferred_element_type=F32)
        mix = mix + jnp.dot(c_ref[...].astype(BF16), wout_ref[a_w:, :], preferred_element_type=F32)
        st["acc"] = x_ref[...] + mix
        st["h"] = _rms_rows(st["acc"], gmlp_ref[...]).astype(BF16)

    def ff(lo):
        z = jnp.dot(st["h"], w1_ref[:, lo:lo + ff_chunk], preferred_element_type=F32)
        z = jnp.maximum(z, 0.0)
        z = (z * z).astype(BF16)
        st["acc"] = st["acc"] + jnp.dot(z, w2_ref[lo:lo + ff_chunk, :], preferred_element_type=F32)

    def store():
        y_ref[...] = st["acc"]

    return [project] + [functools.partial(ff, lo) for lo in range(0, d_ff, ff_chunk)] + [store]


def _finish_kernel(x_ref, a_ref, c_ref, wout_ref, gmlp_ref, w1_ref, w2_ref, y_ref, *, ff_chunk):
    for part in _finish_parts(x_ref, a_ref, c_ref, wout_ref, gmlp_ref, w1_ref, w2_ref, y_ref, ff_chunk):
        part()


def _finish_sample_kernel(pt_ref, x_ref, a_ref, c_ref, wout_ref, gmlp_ref, w1_ref, w2_ref,
                          q_ref, kn_ref, vn_ref, lq1_ref, lk1_ref, lq2_ref, lk2_ref, sub_ref, ckt_hbm, cv_hbm,
                          y_ref, os_ref, kbuf, vbuf, ksem, vsem,
                          *, ff_chunk, n_heads, dqk, page, ch, n_chunks, lam_init):
    step = pl.program_id(0)
    seqs_per_step = q_ref.shape[0]
    t_new = kn_ref.shape[1]
    dv = q_ref.shape[-1] // n_heads
    stream = _SampleStream(pt_ref, ckt_hbm, cv_hbm, kbuf, vbuf, ksem, vsem,
                           n_seqs=pl.num_programs(0) * seqs_per_step, n_heads=n_heads, dqk=dqk, dv=dv,
                           page=page, ch=ch, n_chunks=n_chunks, t_new=t_new)

    @pl.when(step == 0)
    def _():
        stream.prologue()

    parts = _finish_parts(x_ref, a_ref, c_ref, wout_ref, gmlp_ref, w1_ref, w2_ref, y_ref, ff_chunk)
    lam = _lam_value(lq1_ref, lk1_ref, lq2_ref, lk2_ref, lam_init)
    n_done = 0
    for k in range(seqs_per_step):
        seq = step * seqs_per_step + k
        qbd, state = stream.init_state(q_ref[k].astype(F32), kn_ref[k].astype(F32), vn_ref[k])
        for chunk in range(n_chunks):
            stream.wait(chunk % 2)
            state = stream.consume(chunk % 2, qbd, state)
            stream.start_after(seq, chunk)
            n_done += 1
            if n_done % 2 == 1 and parts:
                parts.pop(0)()

        def store(h, o, k=k):
            os_ref[k, :, h * dv:(h + 1) * dv] = o

        stream.finalize(state, lam, sub_ref[...], lam_init, store)
    for part in parts:
        part()


def _finish(x2d, a2d, c2d, wout_b, gmlp, w1_b, w2_b, *, tm, ff_chunk):
    t, d = x2d.shape
    row = lambda width: pl.BlockSpec((tm, width), lambda i: (i, 0))
    return pl.pallas_call(
        functools.partial(_finish_kernel, ff_chunk=ff_chunk),
        grid=(t // tm,),
        in_specs=[row(d), row(a2d.shape[1]), row(c2d.shape[1]), _const_spec(wout_b.shape),
                  _const_spec(gmlp.shape), _const_spec(w1_b.shape), _const_spec(w2_b.shape)],
        out_specs=row(d),
        out_shape=jax.ShapeDtypeStruct((t, d), F32),
        compiler_params=pltpu.CompilerParams(dimension_semantics=("arbitrary",),
                                             vmem_limit_bytes=VMEM_LIMIT_BYTES),
        name="finish",
    )(x2d, a2d, c2d, wout_b, gmlp, w1_b, w2_b)


def _finish_sample(x2d, a2d, c2d, wout_b, gmlp, w1_b, w2_b, page_table, q2, kn, vn, lams, subln, cache_kt, cache_v,
                   *, tm, ff_chunk, n_heads, dqk, ch, lam_init):
    t, d = x2d.shape
    n_seqs, rows2, a = q2.shape
    t_new = rows2 // 2
    n_pool, _, page = cache_kt.shape
    dv = cache_v.shape[-1]
    n_pages = page_table.shape[1]
    n_steps = t // tm
    assert n_pages % ch == 0 and n_seqs % n_steps == 0
    n_chunks = n_pages // ch
    seqs_per_step = n_seqs // n_steps
    kern = functools.partial(_finish_sample_kernel, ff_chunk=ff_chunk, n_heads=n_heads, dqk=dqk, page=page,
                             ch=ch, n_chunks=n_chunks, lam_init=lam_init)
    row = lambda width: pl.BlockSpec((tm, width), lambda i, pt: (i, 0))
    per_seq = lambda r: pl.BlockSpec((seqs_per_step, r, a), lambda i, pt: (i, 0, 0))
    const = lambda x: pl.BlockSpec(x.shape, lambda i, pt: (0,) * x.ndim, pipeline_mode=pl.Buffered(1))
    grid_spec = pltpu.PrefetchScalarGridSpec(
        num_scalar_prefetch=1,
        grid=(n_steps,),
        in_specs=[row(d), row(a2d.shape[1]), row(c2d.shape[1]), const(wout_b), const(gmlp), const(w1_b),
                  const(w2_b), per_seq(rows2), per_seq(t_new), per_seq(t_new)]
        + [const(x) for x in lams] + [const(subln)]
        + [pl.BlockSpec(memory_space=pl.ANY), pl.BlockSpec(memory_space=pl.ANY)],
        out_specs=[row(d), per_seq(t_new)],
        scratch_shapes=[pltpu.VMEM((2, a, ch * page), F32), pltpu.VMEM((2, ch * page * n_heads, dv), F32),
                        pltpu.SemaphoreType.DMA((2,)), pltpu.SemaphoreType.DMA((2,))],
    )
    return pl.pallas_call(
        kern,
        grid_spec=grid_spec,
        out_shape=[jax.ShapeDtypeStruct((t, d), F32), jax.ShapeDtypeStruct((n_seqs, t_new, a), F32)],
        compiler_params=pltpu.CompilerParams(dimension_semantics=("arbitrary",),
                                             vmem_limit_bytes=VMEM_LIMIT_BYTES),
        name="finish_sample",
    )(page_table, x2d, a2d, c2d, wout_b, gmlp, w1_b, w2_b, q2, kn, vn, *lams, subln, cache_kt, cache_v)


def kernel(x_prompt, x_sample, cache_k, cache_v, state_conv, page_table, norm_mix, w_in, q_norm, k_norm,
           lambda_q1, lambda_k1, lambda_q2, lambda_k2, subln, conv_w, w_out, norm_mlp, w_ff1, w_ff2):
    depth, n_pool, page, n_heads, _, dqk = cache_k.shape
    dv = cache_v.shape[-1]
    attn_w = n_heads * dv
    conv_width = state_conv.shape[-1]
    conv_k = conv_w.shape[1]
    batch, seq, d_model = x_prompt.shape
    dec_batch, dec_seq, _ = x_sample.shape
    q_scale = dqk ** -0.5
    n_groups = attn_w // dqk

    gmat = (jnp.kron(jnp.eye(n_groups, dtype=F32), jnp.ones((dqk, dqk), F32)) / dqk).astype(BF16)

    xp = x_prompt.reshape(batch * seq, d_model)
    xs = x_sample.reshape(dec_batch * dec_seq, d_model)
    outs = {name: [] for name in ("kp", "vp", "cp", "ks", "vs", "cs")}
    for l in range(depth):
        lam_init = _lambda_init(l)
        win_b = w_in[l].astype(BF16)
        wout_b = w_out[l].astype(BF16)
        w1_b = w_ff1[l].astype(BF16)
        w2_b = w_ff2[l].astype(BF16)
        gmix = norm_mix[l][None]
        gmlp = norm_mlp[l][None]
        gq = jnp.tile(q_norm[l], n_groups)[None]
        gk = jnp.tile(k_norm[l], n_groups)[None]
        lams = [lambda_q1[l][None], lambda_k1[l][None], lambda_q2[l][None], lambda_k2[l][None]]
        sub = subln[l][None]
        cw = conv_w[l]
        pre = dict(attn_w=attn_w, conv_w=conv_width, n_heads=n_heads, q_scale=q_scale)
        fin = functools.partial(_finish, ff_chunk=1024)

        tm_p, tk_p = 512, 256
        wkt_b = w_in[l][:, attn_w:2 * attn_w].T.astype(BF16)
        gkt = jnp.broadcast_to(gk.reshape(attn_w, 1), (attn_w, tm_p))
        qb, kt, ktb, v_il, vb, c, c_state = _prestage_prompt(
            xp, gmix, win_b, gq, wkt_b, gkt, gmat, cw, tm=tm_p, tk=tk_p, seq_len=seq, **pre)
        a_p = _prompt_attention(qb.reshape(batch, seq, attn_w), ktb, vb.reshape(batch, seq, attn_w), lams, sub,
                                n_heads=n_heads, dqk=dqk, lam_init=lam_init)
        outs["kp"].append(jnp.transpose(kt.reshape(batch, n_heads, 2, dqk, seq), (0, 4, 1, 2, 3)))
        outs["vp"].append(v_il.reshape(batch, seq, n_heads, dv))
        outs["cp"].append(c_state)

        st = state_conv[l]
        hist = jnp.stack([
            jnp.concatenate([st[:, conv_k - 1 - s:, :],
                             jnp.zeros((dec_batch, dec_seq - s, conv_width), F32)], axis=1)
            .reshape(dec_batch * dec_seq, conv_width)
            for s in range(1, conv_k)])
        ts = dec_batch * dec_seq
        qb, k, v, kb, c_s, u = _prestage_sample(xs, gmix, win_b, gq, gk, gmat, cw, hist, seq_len=dec_seq, **pre)
        q2 = jnp.tile(qb.reshape(dec_batch, dec_seq, attn_w), (1, 2, 1))

        xp, a_s = _finish_sample(
            xp, a_p.reshape(batch * seq, attn_w), c, wout_b, gmlp, w1_b, w2_b,
            page_table, q2, kb.reshape(dec_batch, dec_seq, attn_w), v.reshape(dec_batch, dec_seq, attn_w),
            lams, sub, jnp.transpose(cache_k[l], (0, 2, 3, 4, 1)).reshape(n_pool, attn_w, page),
            cache_v[l].reshape(n_pool, page * n_heads, dv),
            tm=256, ff_chunk=1024, n_heads=n_heads, dqk=dqk, ch=16, lam_init=lam_init)
        xs = fin(xs, a_s.reshape(ts, attn_w), c_s, wout_b, gmlp, w1_b, w2_b, tm=256)
        outs["ks"].append(k.reshape(dec_batch, dec_seq, n_heads, 2, dqk))
        outs["vs"].append(v.reshape(dec_batch, dec_seq, n_heads, dv))
        outs["cs"].append(u.reshape(dec_batch, dec_seq, conv_width)[:, dec_seq - (conv_k - 1):, :])

    return (xp.reshape(batch, seq, d_model), xs.reshape(dec_batch, dec_seq, d_model),
            jnp.stack(outs["kp"]), jnp.stack(outs["vp"]), jnp.stack(outs["cp"]),
            jnp.stack(outs["ks"]), jnp.stack(outs["vs"]), jnp.stack(outs["cs"]))
```

```python
import functools
import math

import jax
import jax.numpy as jnp
from jax import lax
from jax.experimental import pallas as pl
from jax.experimental.pallas import tpu as pltpu

F32 = jnp.float32
BF16 = jnp.bfloat16
EPS = 1e-6
NEG_INF = -1e30

V7X_VMEM_BYTES = 64 * 1024 * 1024
V7X_SUBLANES = 8
VMEM_LIMIT_BYTES = V7X_VMEM_BYTES - 8 * 1024 * 1024


def _lambda_init(layer):
    return 0.8 - 0.6 * math.exp(-0.3 * layer)


def _rms_rows(x, g):
    ms = jnp.mean(x * x, axis=-1, keepdims=True)
    return x * lax.rsqrt(ms + EPS) * g


def _const_spec(shape):
    zeros = (0,) * len(shape)
    return pl.BlockSpec(shape, lambda *_: zeros, pipeline_mode=pl.Buffered(1))


def _lam_value(lq1_ref, lk1_ref, lq2_ref, lk2_ref, lam_init):
    s1 = jnp.sum(lq1_ref[...] * lk1_ref[...], axis=-1, keepdims=True)
    s2 = jnp.sum(lq2_ref[...] * lk2_ref[...], axis=-1, keepdims=True)
    return jnp.exp(s1) - jnp.exp(s2) + lam_init


def _prestage_kernel(*refs, attn_w, conv_w, n_heads, seq_tiles, seq_len, q_scale, sample):
    if sample:
        (x_ref, gmix_ref, win_ref, gq_ref, gk_ref, gmat_ref, cw_ref, hist_ref,
         qb_ref, k_ref, v_ref, kb_ref, c_ref, u_ref, ubuf) = refs
    else:
        (x_ref, gmix_ref, win_ref, wqt_ref, gqt_ref, wkt_ref, gkt_ref, gmat_ref, cw_ref,
         qtb_ref, kt_ref, kb_ref, vil_ref, vtb_ref, c_ref, cs_ref, ubuf) = refs
    tm = x_ref.shape[0]
    conv_k = cw_ref.shape[0]
    a, c = attn_w, conv_w
    dv = a // n_heads

    h = _rms_rows(x_ref[...], gmix_ref[...]).astype(BF16)
    base = 0 if sample else 2 * a

    def proj(lo, width):
        return jnp.dot(h, win_ref[:, lo - base:lo - base + width], preferred_element_type=F32)

    gmat = gmat_ref[...]
    v = proj(2 * a, a)
    if sample:
        def group_rms(z, g):
            msq = jnp.dot((z * z).astype(BF16), gmat, preferred_element_type=F32)
            return z * lax.rsqrt(msq + EPS) * g

        q = group_rms(proj(0, a), gq_ref[...])
        qb_ref[...] = (q * q_scale).astype(BF16)
        k = group_rms(proj(a, a), gk_ref[...])
        k_ref[...] = k
        kb_ref[...] = k.astype(BF16)
        v_ref[...] = v
    else:
        def group_rms_t(wt_ref, gt_ref):
            zt = lax.dot_general(wt_ref[...], h, (((1,), (1,)), ((), ())), preferred_element_type=F32)
            msq = jnp.dot(gmat, (zt * zt).astype(BF16), preferred_element_type=F32)
            return zt * lax.rsqrt(msq + EPS) * gt_ref[...]

        qtb_ref[0] = (group_rms_t(wqt_ref, gqt_ref) * q_scale).astype(BF16)
        kt = group_rms_t(wkt_ref, gkt_ref)
        kt_ref[0] = kt
        kb_ref[...] = kt.T.astype(BF16)
        for hh in range(n_heads):
            vil_ref[pl.ds(hh, tm, stride=n_heads), :] = v[:, hh * dv:(hh + 1) * dv]
        vt = v.T
        tk = vtb_ref.shape[-1]
        for t in range(tm // tk):
            vtb_ref[0, t] = vt[:, t * tk:(t + 1) * tk].astype(BF16)
    b_gate = proj(3 * a, c)
    u = proj(3 * a + c, c) * proj(3 * a + 2 * c, c)

    halo = V7X_SUBLANES
    if sample:
        ubuf[0:halo, :] = jnp.zeros((halo, c), F32)
    else:
        first = pl.program_id(0) % seq_tiles == 0

        @pl.when(first)
        def _():
            ubuf[0:halo, :] = jnp.zeros((halo, c), F32)

        @pl.when(jnp.logical_not(first))
        def _():
            ubuf[0:halo, :] = ubuf[tm:tm + halo, :]

    ubuf[halo:halo + tm, :] = u
    if sample:
        t_pos = lax.broadcasted_iota(jnp.int32, (tm, 1), 0) % seq_len
    y = None
    for j in range(conv_k):
        shift = conv_k - 1 - j
        if shift == 0:
            tap = u
        else:
            tap = ubuf[halo - shift:halo - shift + tm, :]
            if sample:
                tap = jnp.where(t_pos >= shift, tap, hist_ref[shift - 1])
        term = cw_ref[j:j + 1, :] * tap
        y = term if y is None else y + term
    c_ref[...] = b_gate * y
    if sample:
        u_ref[...] = u
    else:
        cs_ref[0] = u[tm - (conv_k - 1):tm, :]


def _prestage_sample(x2d, gmix, win_b, gq, gk, gmat, cw, hist, *, attn_w, conv_w, n_heads, seq_len, q_scale):
    t, d = x2d.shape
    assert t % seq_len == 0
    args = [x2d, gmix, win_b, gq, gk, gmat, cw, hist]
    out_shape = [jax.ShapeDtypeStruct((t, attn_w), BF16), jax.ShapeDtypeStruct((t, attn_w), F32),
                 jax.ShapeDtypeStruct((t, attn_w), F32), jax.ShapeDtypeStruct((t, attn_w), BF16),
                 jax.ShapeDtypeStruct((t, conv_w), F32), jax.ShapeDtypeStruct((t, conv_w), F32)]
    kern = functools.partial(_prestage_kernel, attn_w=attn_w, conv_w=conv_w, n_heads=n_heads, seq_tiles=1,
                             seq_len=seq_len, q_scale=q_scale, sample=True)
    return pl.pallas_call(
        kern,
        grid=(1,),
        in_specs=[_const_spec(x.shape) for x in args],
        out_specs=[pl.BlockSpec(s.shape, lambda i: (0, 0)) for s in out_shape],
        out_shape=out_shape,
        scratch_shapes=[pltpu.VMEM((t + 2 * V7X_SUBLANES, conv_w), F32)],
        compiler_params=pltpu.CompilerParams(dimension_semantics=("arbitrary",),
                                             vmem_limit_bytes=VMEM_LIMIT_BYTES),
        name="prestage_sample",
    )(*args)


def _prestage_prompt(x2d, gmix, win_b, wqt_b, gqt, wkt_b, gkt, gmat, cw,
                     *, attn_w, conv_w, n_heads, tm, tk, seq_len, q_scale):
    t, d = x2d.shape
    conv_k = cw.shape[0]
    dv = attn_w // n_heads
    assert seq_len % tm == 0 and tm % tk == 0 and gkt.shape == (attn_w, tm) and gqt.shape == (attn_w, tm)
    seq_tiles = seq_len // tm
    n_seq = t // seq_len
    row = lambda width: pl.BlockSpec((tm, width), lambda i: (i, 0))
    col = pl.BlockSpec((1, attn_w, tm), lambda i: (i // seq_tiles, 0, i % seq_tiles))
    args = [x2d, gmix, win_b, wqt_b, gqt, wkt_b, gkt, gmat, cw]
    in_specs = [row(d)] + [_const_spec(x.shape) for x in args[1:]]
    out_shape = [jax.ShapeDtypeStruct((n_seq, attn_w, seq_len), BF16),
                 jax.ShapeDtypeStruct((n_seq, attn_w, seq_len), F32),
                 jax.ShapeDtypeStruct((t, attn_w), BF16),
                 jax.ShapeDtypeStruct((t * n_heads, dv), F32),
                 jax.ShapeDtypeStruct((n_seq, seq_len // tk, attn_w, tk), BF16),
                 jax.ShapeDtypeStruct((t, conv_w), F32),
                 jax.ShapeDtypeStruct((n_seq, conv_k - 1, conv_w), F32)]
    out_specs = [col, col, row(attn_w),
                 pl.BlockSpec((tm * n_heads, dv), lambda i: (i, 0)),
                 pl.BlockSpec((1, tm // tk, attn_w, tk), lambda i: (i // seq_tiles, i % seq_tiles, 0, 0)),
                 row(conv_w),
                 pl.BlockSpec((1, conv_k - 1, conv_w), lambda i: (i // seq_tiles, 0, 0))]
    kern = functools.partial(_prestage_kernel, attn_w=attn_w, conv_w=conv_w, n_heads=n_heads,
                             seq_tiles=seq_tiles, seq_len=seq_len, q_scale=q_scale, sample=False)
    return pl.pallas_call(
        kern,
        grid=(t // tm,),
        in_specs=in_specs,
        out_specs=out_specs,
        out_shape=out_shape,
        scratch_shapes=[pltpu.VMEM((tm + 2 * V7X_SUBLANES, conv_w), F32)],
        compiler_params=pltpu.CompilerParams(dimension_semantics=("arbitrary",),
                                             vmem_limit_bytes=VMEM_LIMIT_BYTES),
        name="prestage_prompt",
    )(*args)


def _prompt_attn_parts(i, qt_ref, k_ref, vt_ref, sub_ref, o_ref, lam, *, n_heads, dqk, lam_init):
    tq = qt_ref.shape[-1]
    tk = vt_ref.shape[-1]
    assert tq == tk
    dv = qt_ref.shape[1] // n_heads
    st = {}

    def prepare():
        row = lax.broadcasted_iota(jnp.int32, (dv, tq), 0)
        qst = []
        for h in range(n_heads):
            qh = qt_ref[0, h * dv:(h + 1) * dv, :]
            zero = jnp.zeros_like(qh)
            qst.append(jnp.concatenate([jnp.where(row < dqk, qh, zero), jnp.where(row >= dqk, qh, zero)],
                                       axis=1))
        st["qst"] = qst

    def update(j, carry, masked):
        qst = st["qst"]
        start = pl.multiple_of(j * tk, tk)

        def score(h):
            kj = k_ref[0, pl.ds(start, tk), h * dv:(h + 1) * dv]
            s = jnp.dot(kj, qst[h], preferred_element_type=F32)
            if masked:
                key = lax.broadcasted_iota(jnp.int32, s.shape, 0)
                qry = lax.broadcasted_iota(jnp.int32, s.shape, 1) % tq
                s = jnp.where(qry >= key, s, NEG_INF)
            return s

        def softmax(h, s):
            m, l, _ = carry[h]
            m_new = jnp.maximum(m, jnp.max(s, axis=0, keepdims=True))
            p = jnp.exp(s - m_new)
            alpha = jnp.exp(m - m_new)
            l = alpha * l + jnp.sum(p, axis=0, keepdims=True)
            return m_new, l, alpha, p.astype(BF16)

        def weigh(h, m_new, l, alpha, p):
            vtj = vt_ref[0, j, h * dv:(h + 1) * dv, :]
            acc = alpha * carry[h][2] + jnp.dot(vtj, p, preferred_element_type=F32)
            return m_new, l, acc

        scores = [score(h) for h in range(n_heads)]
        probs = [softmax(h, s) for h, s in enumerate(scores)]
        return tuple(weigh(h, *pr) for h, pr in enumerate(probs))

    def below_diagonal():
        init = tuple((jnp.full((1, 2 * tq), NEG_INF, F32), jnp.zeros((1, 2 * tq), F32),
                      jnp.zeros((dv, 2 * tq), F32)) for _ in range(n_heads))
        st["carry"] = lax.fori_loop(0, i, lambda j, c: update(j, c, False), init)

    def diagonal():
        st["carry"] = update(i, st["carry"], True)

    def finish():
        for h in range(n_heads):
            m, l, acc = st["carry"][h]
            ot = acc[:, :tq] / l[:, :tq] - lam * (acc[:, tq:] / l[:, tq:])
            o_ref[0, :, h * dv:(h + 1) * dv] = _rms_rows(ot.T, sub_ref[...]) * (1.0 - lam_init)

    return [prepare, below_diagonal, diagonal, finish]


def _stream_sequences(stream, first_seq, q_ref, kn_ref, vn_ref, os_ref, lam, sub, lam_init, parts):
    dv = stream.dv
    for k in range(q_ref.shape[0]):
        qbd, state = stream.init_state(q_ref[k].astype(F32), kn_ref[k].astype(F32), vn_ref[k])
        for chunk in range(stream.n_chunks):
            stream.wait(chunk % 2)
            state = stream.consume(chunk % 2, qbd, state)
            stream.start_after(first_seq + k, chunk)
            if parts:
                parts.pop(0)()

        def store(h, o, k=k):
            os_ref[k, :, h * dv:(h + 1) * dv] = o

        stream.finalize(state, lam, sub, lam_init, store)
    while parts:
        parts.pop(0)()


def _prompt_attn_kernel(pt_ref, qt_ref, k_ref, vt_ref, lq1_ref, lk1_ref, lq2_ref, lk2_ref, sub_ref,
                        q_ref, kn_ref, vn_ref, ckt_hbm, cv_hbm, o_ref, os_ref, kbuf, vbuf, ksem, vsem,
                        *, n_heads, dqk, page, ch, n_chunks, lam_init):
    step = pl.program_id(0) * pl.num_programs(1) + pl.program_id(1)
    n_steps = pl.num_programs(0) * pl.num_programs(1)
    seqs_per_step = q_ref.shape[0]
    stream = _SampleStream(pt_ref, ckt_hbm, cv_hbm, kbuf, vbuf, ksem, vsem, n_seqs=n_steps * seqs_per_step,
                           n_heads=n_heads, dqk=dqk, dv=q_ref.shape[-1] // n_heads, page=page, ch=ch,
                           n_chunks=n_chunks, t_new=kn_ref.shape[1])

    @pl.when(step == 0)
    def _():
        stream.prologue()

    lam = _lam_value(lq1_ref, lk1_ref, lq2_ref, lk2_ref, lam_init)
    parts = _prompt_attn_parts(pl.program_id(1), qt_ref, k_ref, vt_ref, sub_ref, o_ref, lam,
                               n_heads=n_heads, dqk=dqk, lam_init=lam_init)
    _stream_sequences(stream, step * seqs_per_step, q_ref, kn_ref, vn_ref, os_ref, lam, sub_ref[...], lam_init,
                      parts)


def _stream_specs(q2, page, ch, n_heads, dv, index):
    n_seqs, rows2, a = q2.shape
    per_seq = lambda r, n: pl.BlockSpec((n, r, a), lambda *ids: (index(*ids[:-1]), 0, 0))
    any_spec = pl.BlockSpec(memory_space=pl.ANY)
    scratch = [pltpu.VMEM((2, a, ch * page), F32), pltpu.VMEM((2, ch * page * n_heads, dv), F32),
               pltpu.SemaphoreType.DMA((2,)), pltpu.SemaphoreType.DMA((2,))]
    return per_seq, any_spec, scratch


def _prompt_attention(qtb, kb, vtb, lams, subln, page_table, q2, kn, vn, cache_kt, cache_v,
                      *, n_heads, dqk, ch, lam_init):
    b, a, s = qtb.shape
    nk, tk = vtb.shape[1], vtb.shape[3]
    nq = s // tk
    n_seqs, rows2, _ = q2.shape
    t_new = rows2 // 2
    page = cache_kt.shape[-1]
    dv = cache_v.shape[-1]
    n_pages = page_table.shape[1]
    assert n_pages % ch == 0 and n_seqs % (b * nq) == 0
    seqs_per_step = n_seqs // (b * nq)
    kern = functools.partial(_prompt_attn_kernel, n_heads=n_heads, dqk=dqk, page=page, ch=ch,
                             n_chunks=n_pages // ch, lam_init=lam_init)
    per_seq, any_spec, scratch = _stream_specs(q2, page, ch, n_heads, dv, lambda bi, i: bi * nq + i)
    const = lambda x: pl.BlockSpec(x.shape, lambda bi, i, pt: (0,) * x.ndim, pipeline_mode=pl.Buffered(1))
    grid_spec = pltpu.PrefetchScalarGridSpec(
        num_scalar_prefetch=1,
        grid=(b, nq),
        in_specs=[pl.BlockSpec((1, a, tk), lambda bi, i, pt: (bi, 0, i)),
                  pl.BlockSpec((1, s, a), lambda bi, i, pt: (bi, 0, 0)),
                  pl.BlockSpec((1, nk, a, tk), lambda bi, i, pt: (bi, 0, 0, 0))]
        + [const(x) for x in lams] + [const(subln)]
        + [per_seq(rows2, seqs_per_step), per_seq(t_new, seqs_per_step), per_seq(t_new, seqs_per_step),
           any_spec, any_spec],
        out_specs=[pl.BlockSpec((1, tk, a), lambda bi, i, pt: (bi, i, 0)), per_seq(t_new, seqs_per_step)],
        scratch_shapes=scratch,
    )
    return pl.pallas_call(
        kern,
        grid_spec=grid_spec,
        out_shape=[jax.ShapeDtypeStruct((b, s, a), F32), jax.ShapeDtypeStruct((n_seqs, t_new, a), F32)],
        compiler_params=pltpu.CompilerParams(dimension_semantics=("arbitrary",) * 2,
                                             vmem_limit_bytes=VMEM_LIMIT_BYTES),
        name="prompt_attn_sample",
    )(page_table, qtb, kb, vtb, *lams, subln, q2, kn, vn, cache_kt, cache_v)


def _k_page_copy(ckt_hbm, kbuf, sem, page_idx, slot, p, page):
    return pltpu.make_async_copy(ckt_hbm.at[page_idx], kbuf.at[slot, :, pl.ds(p * page, page)], sem.at[slot])


def _v_page_copy(cv_hbm, vbuf, sem, page_idx, slot, p, rows):
    return pltpu.make_async_copy(cv_hbm.at[page_idx], vbuf.at[slot, pl.ds(p * rows, rows), :], sem.at[slot])


class _SampleStream:
    def __init__(self, pt_ref, ckt_hbm, cv_hbm, kbuf, vbuf, ksem, vsem, *, n_seqs, n_heads, dqk, dv, page, ch,
                 n_chunks, t_new):
        assert n_chunks % 2 == 0 and n_heads % 2 == 0
        self.pt_ref, self.ckt_hbm, self.cv_hbm = pt_ref, ckt_hbm, cv_hbm
        self.kbuf, self.vbuf, self.ksem, self.vsem = kbuf, vbuf, ksem, vsem
        self.n_seqs, self.n_heads, self.dqk, self.dv = n_seqs, n_heads, dqk, dv
        self.page, self.ch, self.n_chunks, self.t_new = page, ch, n_chunks, t_new

    def start(self, seq, chunk, slot):
        for p in range(self.ch):
            pg = self.pt_ref[seq, chunk * self.ch + p]
            _k_page_copy(self.ckt_hbm, self.kbuf, self.ksem, pg, slot, p, self.page).start()
            _v_page_copy(self.cv_hbm, self.vbuf, self.vsem, pg, slot, p, self.page * self.n_heads).start()

    def wait(self, slot):
        for p in range(self.ch):
            _k_page_copy(self.ckt_hbm, self.kbuf, self.ksem, 0, slot, p, self.page).wait()
            _v_page_copy(self.cv_hbm, self.vbuf, self.vsem, 0, slot, p, self.page * self.n_heads).wait()

    def prologue(self):
        self.start(0, 0, 0)
        self.start(0, 1, 1)

    def start_after(self, seq, chunk):
        slot = chunk % 2
        nxt = chunk + 2
        if nxt < self.n_chunks:
            self.start(seq, nxt, slot)
        else:
            @pl.when(seq + 1 < self.n_seqs)
            def _():
                self.start(seq + 1, nxt - self.n_chunks, slot)

    def init_state(self, q2, kn, vn):
        t_new, dv, dqk = self.t_new, self.dv, self.dqk
        rows2 = 2 * t_new
        lane = lax.broadcasted_iota(jnp.int32, (rows2, dv), 1)
        rowi = lax.broadcasted_iota(jnp.int32, (rows2, dv), 0)
        lane_lo = jnp.where(rowi < t_new, 0, dqk)
        comp_mask = jnp.logical_and(lane >= lane_lo, lane < lane_lo + dqk)
        t_row = lax.broadcasted_iota(jnp.int32, (rows2, 1), 0) % t_new
        zero = jnp.zeros((rows2, dv), F32)
        qbd, state = [], []
        for pair in range(self.n_heads // 2):
            per_head = []
            for h in (2 * pair, 2 * pair + 1):
                qf = jnp.where(comp_mask, q2[:, h * dv:(h + 1) * dv], 0.0)
                s_new = [jnp.sum(qf * kn[j:j + 1, h * dv:(h + 1) * dv], axis=-1, keepdims=True)
                         for j in range(t_new)]
                valid = [t_row >= j for j in range(t_new)]
                m = s_new[0]
                for j in range(1, t_new):
                    m = jnp.maximum(m, jnp.where(valid[j], s_new[j], NEG_INF))
                l = jnp.zeros((rows2, 1), F32)
                acc = zero
                for j in range(t_new):
                    pj = jnp.where(valid[j], jnp.exp(s_new[j] - m), 0.0)
                    l = l + pj
                    acc = acc + pj * vn[j:j + 1, h * dv:(h + 1) * dv]
                per_head.append((qf, m, l, acc))
            (q0, m0, l0, a0), (q1, m1, l1, a1) = per_head
            qbd.append(jnp.concatenate([jnp.concatenate([q0, zero], axis=1),
                                        jnp.concatenate([zero, q1], axis=1)], axis=0))
            state.append((jnp.concatenate([m0, m1], axis=0), jnp.concatenate([l0, l1], axis=0),
                          jnp.concatenate([jnp.concatenate([a0, zero], axis=1),
                                           jnp.concatenate([zero, a1], axis=1)], axis=0)))
        return qbd, state

    def consume(self, slot, qbd, state):
        dv = self.dv
        tokens = self.ch * self.page
        out = []
        for pair, (m, l, acc) in enumerate(state):
            h0, h1 = 2 * pair, 2 * pair + 1
            kt = self.kbuf[slot, h0 * dv:(h1 + 1) * dv, :]
            s = jnp.dot(qbd[pair], kt, preferred_element_type=F32)
            m_new = jnp.maximum(m, jnp.max(s, axis=-1, keepdims=True))
            p = jnp.exp(s - m_new)
            alpha = jnp.exp(m - m_new)
            l = alpha * l + jnp.sum(p, axis=-1, keepdims=True)
            v2 = jnp.concatenate([self.vbuf[slot, pl.ds(h0, tokens, stride=self.n_heads), :],
                                  self.vbuf[slot, pl.ds(h1, tokens, stride=self.n_heads), :]], axis=1)
            acc = alpha * acc + jnp.dot(p, v2, preferred_element_type=F32)
            out.append((m_new, l, acc))
        return out

    def finalize(self, state, lam, sub, lam_init, store):
        t_new, dv = self.t_new, self.dv
        rows2 = 2 * t_new
        for pair, (m, l, acc) in enumerate(state):
            for k in range(2):
                a_h = acc[k * rows2:(k + 1) * rows2, k * dv:(k + 1) * dv]
                l_h = l[k * rows2:(k + 1) * rows2]
                o = a_h[:t_new] / l_h[:t_new] - lam * (a_h[t_new:] / l_h[t_new:])
                store(2 * pair + k, _rms_rows(o, sub) * (1.0 - lam_init))


def _finish_parts(x_ref, a_ref, c_ref, wout_ref, gmlp_ref, w1_ref, w2_ref, y_ref, ff_chunk):
    a_w = a_ref.shape[1]
    d_ff = w1_ref.shape[1]
    st = {}

    def project():
        mix = jnp.dot(a_ref[...].astype(BF16), wout_ref[0:a_w, :], preferred_element_type=F32)
        mix = mix + jnp.dot(c_ref[...].astype(BF16), wout_ref[a_w:, :], preferred_element_type=F32)
        st["acc"] = x_ref[...] + mix
        st["h"] = _rms_rows(st["acc"], gmlp_ref[...]).astype(BF16)

    def ff(lo):
        z = jnp.dot(st["h"], w1_ref[:, lo:lo + ff_chunk], preferred_element_type=F32)
        z = jnp.maximum(z, 0.0)
        z = (z * z).astype(BF16)
        st["acc"] = st["acc"] + jnp.dot(z, w2_ref[lo:lo + ff_chunk, :], preferred_element_type=F32)

    def store():
        y_ref[...] = st["acc"]

    return [project] + [functools.partial(ff, lo) for lo in range(0, d_ff, ff_chunk)] + [store]


def _finish_kernel(x_ref, a_ref, c_ref, wout_ref, gmlp_ref, w1_ref, w2_ref, y_ref, *, ff_chunk):
    for part in _finish_parts(x_ref, a_ref, c_ref, wout_ref, gmlp_ref, w1_ref, w2_ref, y_ref, ff_chunk):
        part()


def _finish_sample_kernel(pt_ref, x_ref, a_ref, c_ref, wout_ref, gmlp_ref, w1_ref, w2_ref,
                          q_ref, kn_ref, vn_ref, lq1_ref, lk1_ref, lq2_ref, lk2_ref, sub_ref, ckt_hbm, cv_hbm,
                          y_ref, os_ref, kbuf, vbuf, ksem, vsem,
                          *, ff_chunk, n_heads, dqk, page, ch, n_chunks, lam_init):
    step = pl.program_id(0)
    seqs_per_step = q_ref.shape[0]
    t_new = kn_ref.shape[1]
    dv = q_ref.shape[-1] // n_heads
    stream = _SampleStream(pt_ref, ckt_hbm, cv_hbm, kbuf, vbuf, ksem, vsem,
                           n_seqs=pl.num_programs(0) * seqs_per_step, n_heads=n_heads, dqk=dqk, dv=dv,
                           page=page, ch=ch, n_chunks=n_chunks, t_new=t_new)

    @pl.when(step == 0)
    def _():
        stream.prologue()

    parts = _finish_parts(x_ref, a_ref, c_ref, wout_ref, gmlp_ref, w1_ref, w2_ref, y_ref, ff_chunk)
    lam = _lam_value(lq1_ref, lk1_ref, lq2_ref, lk2_ref, lam_init)
    _stream_sequences(stream, step * seqs_per_step, q_ref, kn_ref, vn_ref, os_ref, lam, sub_ref[...], lam_init,
                      parts)


def _finish(x2d, a2d, c2d, wout_b, gmlp, w1_b, w2_b, *, tm, ff_chunk):
    t, d = x2d.shape
    row = lambda width: pl.BlockSpec((tm, width), lambda i: (i, 0))
    return pl.pallas_call(
        functools.partial(_finish_kernel, ff_chunk=ff_chunk),
        grid=(t // tm,),
        in_specs=[row(d), row(a2d.shape[1]), row(c2d.shape[1]), _const_spec(wout_b.shape),
                  _const_spec(gmlp.shape), _const_spec(w1_b.shape), _const_spec(w2_b.shape)],
        out_specs=row(d),
        out_shape=jax.ShapeDtypeStruct((t, d), F32),
        compiler_params=pltpu.CompilerParams(dimension_semantics=("arbitrary",),
                                             vmem_limit_bytes=VMEM_LIMIT_BYTES),
        name="finish",
    )(x2d, a2d, c2d, wout_b, gmlp, w1_b, w2_b)


def _finish_sample(x2d, a2d, c2d, wout_b, gmlp, w1_b, w2_b, page_table, q2, kn, vn, lams, subln, cache_kt, cache_v,
                   *, tm, ff_chunk, n_heads, dqk, ch, lam_init):
    t, d = x2d.shape
    n_seqs, rows2, a = q2.shape
    t_new = rows2 // 2
    n_pool, _, page = cache_kt.shape
    dv = cache_v.shape[-1]
    n_pages = page_table.shape[1]
    n_steps = t // tm
    assert n_pages % ch == 0 and n_seqs % n_steps == 0
    n_chunks = n_pages // ch
    seqs_per_step = n_seqs // n_steps
    kern = functools.partial(_finish_sample_kernel, ff_chunk=ff_chunk, n_heads=n_heads, dqk=dqk, page=page,
                             ch=ch, n_chunks=n_chunks, lam_init=lam_init)
    row = lambda width: pl.BlockSpec((tm, width), lambda i, pt: (i, 0))
    per_seq, any_spec, scratch = _stream_specs(q2, page, ch, n_heads, dv, lambda i: i)
    const = lambda x: pl.BlockSpec(x.shape, lambda i, pt: (0,) * x.ndim, pipeline_mode=pl.Buffered(1))
    grid_spec = pltpu.PrefetchScalarGridSpec(
        num_scalar_prefetch=1,
        grid=(n_steps,),
        in_specs=[row(d), row(a2d.shape[1]), row(c2d.shape[1]), const(wout_b), const(gmlp), const(w1_b),
                  const(w2_b), per_seq(rows2, seqs_per_step), per_seq(t_new, seqs_per_step),
                  per_seq(t_new, seqs_per_step)]
        + [const(x) for x in lams] + [const(subln), any_spec, any_spec],
        out_specs=[row(d), per_seq(t_new, seqs_per_step)],
        scratch_shapes=scratch,
    )
    return pl.pallas_call(
        kern,
        grid_spec=grid_spec,
        out_shape=[jax.ShapeDtypeStruct((t, d), F32), jax.ShapeDtypeStruct((n_seqs, t_new, a), F32)],
        compiler_params=pltpu.CompilerParams(dimension_semantics=("arbitrary",),
                                             vmem_limit_bytes=VMEM_LIMIT_BYTES),
        name="finish_sample",
    )(page_table, x2d, a2d, c2d, wout_b, gmlp, w1_b, w2_b, q2, kn, vn, *lams, subln, cache_kt, cache_v)


def kernel(x_prompt, x_sample, cache_k, cache_v, state_conv, page_table, norm_mix, w_in, q_norm, k_norm,
           lambda_q1, lambda_k1, lambda_q2, lambda_k2, subln, conv_w, w_out, norm_mlp, w_ff1, w_ff2):
    depth, n_pool, page, n_heads, _, dqk = cache_k.shape
    dv = cache_v.shape[-1]
    attn_w = n_heads * dv
    conv_width = state_conv.shape[-1]
    conv_k = conv_w.shape[1]
    batch, seq, d_model = x_prompt.shape
    dec_batch, dec_seq, _ = x_sample.shape
    q_scale = dqk ** -0.5
    n_groups = attn_w // dqk

    gmat = (jnp.kron(jnp.eye(n_groups, dtype=F32), jnp.ones((dqk, dqk), F32)) / dqk).astype(BF16)

    xp = x_prompt.reshape(batch * seq, d_model)
    xs = x_sample.reshape(dec_batch * dec_seq, d_model)
    outs = {name: [] for name in ("kp", "vp", "cp", "ks", "vs", "cs")}
    for l in range(depth):
        lam_init = _lambda_init(l)
        win_b = w_in[l].astype(BF16)
        wout_b = w_out[l].astype(BF16)
        w1_b = w_ff1[l].astype(BF16)
        w2_b = w_ff2[l].astype(BF16)
        gmix = norm_mix[l][None]
        gmlp = norm_mlp[l][None]
        gq = jnp.tile(q_norm[l], n_groups)[None]
        gk = jnp.tile(k_norm[l], n_groups)[None]
        lams = [lambda_q1[l][None], lambda_k1[l][None], lambda_q2[l][None], lambda_k2[l][None]]
        sub = subln[l][None]
        cw = conv_w[l]
        pre = dict(attn_w=attn_w, conv_w=conv_width, n_heads=n_heads, q_scale=q_scale)
        fin = functools.partial(_finish, ff_chunk=1024)

        tm_p, tk_p = 512, 256
        wqt_b = w_in[l][:, :attn_w].T.astype(BF16)
        wkt_b = w_in[l][:, attn_w:2 * attn_w].T.astype(BF16)
        gqt = jnp.broadcast_to(gq.reshape(attn_w, 1), (attn_w, tm_p))
        gkt = jnp.broadcast_to(gk.reshape(attn_w, 1), (attn_w, tm_p))
        qtb, kt, kb, v_il, vtb, c, c_state = _prestage_prompt(
            xp, gmix, win_b[:, 2 * attn_w:], wqt_b, gqt, wkt_b, gkt, gmat, cw,
            tm=tm_p, tk=tk_p, seq_len=seq, **pre)
        outs["kp"].append(jnp.transpose(kt.reshape(batch, n_heads, 2, dqk, seq), (0, 4, 1, 2, 3)))
        outs["vp"].append(v_il.reshape(batch, seq, n_heads, dv))
        outs["cp"].append(c_state)

        st = state_conv[l]
        hist = jnp.stack([
            jnp.concatenate([st[:, conv_k - 1 - s:, :],
                             jnp.zeros((dec_batch, dec_seq - s, conv_width), F32)], axis=1)
            .reshape(dec_batch * dec_seq, conv_width)
            for s in range(1, conv_k)])
        ts = dec_batch * dec_seq
        qb_s, k, v, kb_s, c_s, u = _prestage_sample(xs, gmix, win_b, gq, gk, gmat, cw, hist, seq_len=dec_seq,
                                                    **pre)
        q2 = jnp.tile(qb_s.reshape(dec_batch, dec_seq, attn_w), (1, 2, 1))
        kn = kb_s.reshape(dec_batch, dec_seq, attn_w)
        vn = v.reshape(dec_batch, dec_seq, attn_w)
        cache_kt = jnp.transpose(cache_k[l], (0, 2, 3, 4, 1)).reshape(n_pool, attn_w, page)
        cache_vr = cache_v[l].reshape(n_pool, page * n_heads, dv)

        n_a = dec_batch // 2
        stream_kw = dict(n_heads=n_heads, dqk=dqk, ch=16, lam_init=lam_init)
        a_p, a_s0 = _prompt_attention(qtb, kb.reshape(batch, seq, attn_w), vtb, lams, sub,
                                      page_table[:n_a], q2[:n_a], kn[:n_a], vn[:n_a], cache_kt, cache_vr,
                                      **stream_kw)
        xp, a_s1 = _finish_sample(xp, a_p.reshape(batch * seq, attn_w), c, wout_b, gmlp, w1_b, w2_b,
                                  page_table[n_a:], q2[n_a:], kn[n_a:], vn[n_a:], lams, sub, cache_kt, cache_vr,
                                  tm=256, ff_chunk=1024, **stream_kw)
        a_s = jnp.concatenate([a_s0, a_s1], axis=0)
        xs = fin(xs, a_s.reshape(ts, attn_w), c_s, wout_b, gmlp, w1_b, w2_b, tm=256)
        outs["ks"].append(k.reshape(dec_batch, dec_seq, n_heads, 2, dqk))
        outs["vs"].append(v.reshape(dec_batch, dec_seq, n_heads, dv))
        outs["cs"].append(u.reshape(dec_batch, dec_seq, conv_width)[:, dec_seq - (conv_k - 1):, :])

    return (xp.reshape(batch, seq, d_model), xs.reshape(dec_batch, dec_seq, d_model),
            jnp.stack(outs["kp"]), jnp.stack(outs["vp"]), jnp.stack(outs["cp"]),
            jnp.stack(outs["ks"]), jnp.stack(outs["vs"]), jnp.stack(outs["cs"]))
```

```python
import functools
import math

import jax
import jax.numpy as jnp
from jax import lax
from jax.experimental import pallas as pl
from jax.experimental.pallas import tpu as pltpu

F32 = jnp.float32
BF16 = jnp.bfloat16
EPS = 1e-6
NEG_INF = -1e30

V7X_VMEM_BYTES = 64 * 1024 * 1024
V7X_SUBLANES = 8
VMEM_LIMIT_BYTES = V7X_VMEM_BYTES - 8 * 1024 * 1024


def _lambda_init(layer):
    return 0.8 - 0.6 * math.exp(-0.3 * layer)


def _rms_rows(x, g):
    ms = jnp.mean(x * x, axis=-1, keepdims=True)
    return x * lax.rsqrt(ms + EPS) * g


def _const_spec(shape):
    zeros = (0,) * len(shape)
    return pl.BlockSpec(shape, lambda *_: zeros, pipeline_mode=pl.Buffered(1))


def _lam_value(lq1_ref, lk1_ref, lq2_ref, lk2_ref, lam_init):
    s1 = jnp.sum(lq1_ref[...] * lk1_ref[...], axis=-1, keepdims=True)
    s2 = jnp.sum(lq2_ref[...] * lk2_ref[...], axis=-1, keepdims=True)
    return jnp.exp(s1) - jnp.exp(s2) + lam_init


def _prestage_parts(refs, *, attn_w, conv_w, n_heads, seq_tiles, seq_len, q_scale, sample):
    if sample:
        (x_ref, gmix_ref, win_ref, gq_ref, gk_ref, gmat_ref, cw_ref, hist_ref,
         qb_ref, k_ref, v_ref, kb_ref, c_ref, u_ref, ubuf) = refs
    else:
        (x_ref, gmix_ref, win_ref, wqt_ref, gqt_ref, wkt_ref, gkt_ref, gmat_ref, cw_ref,
         qtb_ref, kt_ref, kb_ref, vil_ref, vtb_ref, c_ref, cs_ref, ubuf) = refs
    tm = x_ref.shape[0]
    conv_k = cw_ref.shape[0]
    a, c = attn_w, conv_w
    dv = a // n_heads
    base = 0 if sample else 2 * a
    st = {}

    def proj(lo, width):
        return jnp.dot(st["h"], win_ref[:, lo - base:lo - base + width], preferred_element_type=F32)

    def group_rms(z, g):
        msq = jnp.dot((z * z).astype(BF16), gmat_ref[...], preferred_element_type=F32)
        return z * lax.rsqrt(msq + EPS) * g

    def group_rms_t(wt_ref, gt_ref):
        zt = lax.dot_general(wt_ref[...], st["h"], (((1,), (1,)), ((), ())), preferred_element_type=F32)
        msq = jnp.dot(gmat_ref[...], (zt * zt).astype(BF16), preferred_element_type=F32)
        return zt * lax.rsqrt(msq + EPS) * gt_ref[...]

    def queries():
        st["h"] = _rms_rows(x_ref[...], gmix_ref[...]).astype(BF16)
        if sample:
            qb_ref[...] = (group_rms(proj(0, a), gq_ref[...]) * q_scale).astype(BF16)
        else:
            qtb_ref[0] = (group_rms_t(wqt_ref, gqt_ref) * q_scale).astype(BF16)

    def keys():
        if sample:
            k = group_rms(proj(a, a), gk_ref[...])
            k_ref[...] = k
            kb_ref[...] = k.astype(BF16)
        else:
            kt = group_rms_t(wkt_ref, gkt_ref)
            kt_ref[0] = kt
            kb_ref[...] = kt.T.astype(BF16)

    def values():
        v = proj(2 * a, a)
        if sample:
            v_ref[...] = v
        else:
            for hh in range(n_heads):
                vil_ref[pl.ds(hh, tm, stride=n_heads), :] = v[:, hh * dv:(hh + 1) * dv]
            vt = v.T
            tk = vtb_ref.shape[-1]
            for t in range(tm // tk):
                vtb_ref[0, t] = vt[:, t * tk:(t + 1) * tk].astype(BF16)

    def conv():
        b_gate = proj(3 * a, c)
        u = proj(3 * a + c, c) * proj(3 * a + 2 * c, c)
        halo = V7X_SUBLANES
        if sample:
            ubuf[0:halo, :] = jnp.zeros((halo, c), F32)
        else:
            first = pl.program_id(0) % seq_tiles == 0

            @pl.when(first)
            def _():
                ubuf[0:halo, :] = jnp.zeros((halo, c), F32)

            @pl.when(jnp.logical_not(first))
            def _():
                ubuf[0:halo, :] = ubuf[tm:tm + halo, :]

        ubuf[halo:halo + tm, :] = u
        if sample:
            t_pos = lax.broadcasted_iota(jnp.int32, (tm, 1), 0) % seq_len
        y = None
        for j in range(conv_k):
            shift = conv_k - 1 - j
            if shift == 0:
                tap = u
            else:
                tap = ubuf[halo - shift:halo - shift + tm, :]
                if sample:
                    tap = jnp.where(t_pos >= shift, tap, hist_ref[shift - 1])
            term = cw_ref[j:j + 1, :] * tap
            y = term if y is None else y + term
        c_ref[...] = b_gate * y
        if sample:
            u_ref[...] = u
        else:
            cs_ref[0] = u[tm - (conv_k - 1):tm, :]

    return [queries, keys, values, conv]


def _prestage_kernel(*refs, **kw):
    for part in _prestage_parts(refs, **kw):
        part()


def _prestage_stream_kernel(pt_ref, *refs, n_in, n_out, dqk, page, ch, n_chunks, lam_init, **kw):
    pre_in, rest = refs[:n_in], refs[n_in:]
    (lq1_ref, lk1_ref, lq2_ref, lk2_ref, sub_ref, q_ref, kn_ref, vn_ref, ckt_hbm, cv_hbm), rest = rest[:10], rest[10:]
    pre_out, (os_ref, ubuf, kbuf, vbuf, ksem, vsem) = rest[:n_out], rest[n_out:]
    step = pl.program_id(0)
    seqs_per_step = q_ref.shape[0]
    n_heads = kw["n_heads"]
    stream = _SampleStream(pt_ref, ckt_hbm, cv_hbm, kbuf, vbuf, ksem, vsem,
                           n_seqs=pl.num_programs(0) * seqs_per_step, n_heads=n_heads, dqk=dqk,
                           dv=q_ref.shape[-1] // n_heads, page=page, ch=ch, n_chunks=n_chunks,
                           t_new=kn_ref.shape[1])

    @pl.when(step == 0)
    def _():
        stream.prologue()

    parts = _prestage_parts(tuple(pre_in) + tuple(pre_out) + (ubuf,), **kw)
    lam = _lam_value(lq1_ref, lk1_ref, lq2_ref, lk2_ref, lam_init)
    _stream_sequences(stream, step * seqs_per_step, q_ref, kn_ref, vn_ref, os_ref, lam, sub_ref[...], lam_init,
                      parts)


def _prestage_sample(x2d, gmix, win_b, gq, gk, gmat, cw, hist, *, attn_w, conv_w, n_heads, seq_len, q_scale):
    t, d = x2d.shape
    assert t % seq_len == 0
    args = [x2d, gmix, win_b, gq, gk, gmat, cw, hist]
    out_shape = [jax.ShapeDtypeStruct((t, attn_w), BF16), jax.ShapeDtypeStruct((t, attn_w), F32),
                 jax.ShapeDtypeStruct((t, attn_w), F32), jax.ShapeDtypeStruct((t, attn_w), BF16),
                 jax.ShapeDtypeStruct((t, conv_w), F32), jax.ShapeDtypeStruct((t, conv_w), F32)]
    kern = functools.partial(_prestage_kernel, attn_w=attn_w, conv_w=conv_w, n_heads=n_heads, seq_tiles=1,
                             seq_len=seq_len, q_scale=q_scale, sample=True)
    return pl.pallas_call(
        kern,
        grid=(1,),
        in_specs=[_const_spec(x.shape) for x in args],
        out_specs=[pl.BlockSpec(s.shape, lambda i: (0, 0)) for s in out_shape],
        out_shape=out_shape,
        scratch_shapes=[pltpu.VMEM((t + 2 * V7X_SUBLANES, conv_w), F32)],
        compiler_params=pltpu.CompilerParams(dimension_semantics=("arbitrary",),
                                             vmem_limit_bytes=VMEM_LIMIT_BYTES),
        name="prestage_sample",
    )(*args)


def _prestage_prompt(x2d, gmix, win_b, wqt_b, gqt, wkt_b, gkt, gmat, cw,
                     page_table, q2, kn, vn, lams, subln, cache_kt, cache_v,
                     *, attn_w, conv_w, n_heads, dqk, tm, tk, seq_len, q_scale, ch, n_slots, lam_init):
    t, d = x2d.shape
    conv_k = cw.shape[0]
    dv = attn_w // n_heads
    assert seq_len % tm == 0 and tm % tk == 0 and gkt.shape == (attn_w, tm) and gqt.shape == (attn_w, tm)
    seq_tiles = seq_len // tm
    n_seq = t // seq_len
    n_steps = t // tm
    n_seqs, rows2, _ = q2.shape
    t_new = rows2 // 2
    page = cache_kt.shape[-1]
    n_pages = page_table.shape[1]
    assert n_pages % ch == 0 and n_seqs % n_steps == 0
    seqs_per_step = n_seqs // n_steps
    row = lambda width: pl.BlockSpec((tm, width), lambda i, pt: (i, 0))
    col = pl.BlockSpec((1, attn_w, tm), lambda i, pt: (i // seq_tiles, 0, i % seq_tiles))
    const = lambda x: pl.BlockSpec(x.shape, lambda i, pt: (0,) * x.ndim, pipeline_mode=pl.Buffered(1))
    per_seq, any_spec, scratch = _stream_specs(q2, page, ch, n_slots, n_heads, dv, lambda i: i)
    pre_args = [x2d, gmix, win_b, wqt_b, gqt, wkt_b, gkt, gmat, cw]
    stream_args = [*lams, subln, q2, kn, vn, cache_kt, cache_v]
    in_specs = ([row(d)] + [const(x) for x in pre_args[1:]] + [const(x) for x in lams] + [const(subln)]
                + [per_seq(rows2, seqs_per_step), per_seq(t_new, seqs_per_step), per_seq(t_new, seqs_per_step),
                   any_spec, any_spec])
    out_shape = [jax.ShapeDtypeStruct((n_seq, attn_w, seq_len), BF16),
                 jax.ShapeDtypeStruct((n_seq, attn_w, seq_len), F32),
                 jax.ShapeDtypeStruct((t, attn_w), BF16),
                 jax.ShapeDtypeStruct((t * n_heads, dv), F32),
                 jax.ShapeDtypeStruct((n_seq, seq_len // tk, attn_w, tk), BF16),
                 jax.ShapeDtypeStruct((t, conv_w), F32),
                 jax.ShapeDtypeStruct((n_seq, conv_k - 1, conv_w), F32),
                 jax.ShapeDtypeStruct((n_seqs, t_new, attn_w), F32)]
    out_specs = [col, col, row(attn_w),
                 pl.BlockSpec((tm * n_heads, dv), lambda i, pt: (i, 0)),
                 pl.BlockSpec((1, tm // tk, attn_w, tk), lambda i, pt: (i // seq_tiles, i % seq_tiles, 0, 0)),
                 row(conv_w),
                 pl.BlockSpec((1, conv_k - 1, conv_w), lambda i, pt: (i // seq_tiles, 0, 0)),
                 per_seq(t_new, seqs_per_step)]
    kern = functools.partial(_prestage_stream_kernel, n_in=len(pre_args), n_out=len(out_shape) - 1, dqk=dqk,
                             page=page, ch=ch, n_chunks=n_pages // ch, lam_init=lam_init,
                             attn_w=attn_w, conv_w=conv_w, n_heads=n_heads, seq_tiles=seq_tiles,
                             seq_len=seq_len, q_scale=q_scale, sample=False)
    grid_spec = pltpu.PrefetchScalarGridSpec(
        num_scalar_prefetch=1,
        grid=(n_steps,),
        in_specs=in_specs,
        out_specs=out_specs,
        scratch_shapes=[pltpu.VMEM((tm + 2 * V7X_SUBLANES, conv_w), F32)] + scratch,
    )
    return pl.pallas_call(
        kern,
        grid_spec=grid_spec,
        out_shape=out_shape,
        compiler_params=pltpu.CompilerParams(dimension_semantics=("arbitrary",),
                                             vmem_limit_bytes=VMEM_LIMIT_BYTES),
        name="prestage_prompt_sample",
    )(page_table, *pre_args, *stream_args)


def _prompt_attn_parts(i, off, qt_ref, k_ref, vt_ref, sub_ref, o_ref, lam, *, n_heads, dqk, lam_init):
    tq = tk = vt_ref.shape[-1]
    dv = qt_ref.shape[1] // n_heads
    st = {}

    def prepare():
        row = lax.broadcasted_iota(jnp.int32, (dv, tq), 0)
        qst = []
        for h in range(n_heads):
            qh = qt_ref[0, h * dv:(h + 1) * dv, off:off + tq]
            zero = jnp.zeros_like(qh)
            qst.append(jnp.concatenate([jnp.where(row < dqk, qh, zero), jnp.where(row >= dqk, qh, zero)],
                                       axis=1))
        st["qst"] = qst

    def update(j, carry, masked):
        qst = st["qst"]
        start = pl.multiple_of(j * tk, tk)

        def score(h):
            kj = k_ref[0, pl.ds(start, tk), h * dv:(h + 1) * dv]
            s = jnp.dot(kj, qst[h], preferred_element_type=F32)
            if masked:
                key = lax.broadcasted_iota(jnp.int32, s.shape, 0)
                qry = lax.broadcasted_iota(jnp.int32, s.shape, 1) % tq
                s = jnp.where(qry >= key, s, NEG_INF)
            return s

        def softmax(h, s):
            m, l, _ = carry[h]
            m_new = jnp.maximum(m, jnp.max(s, axis=0, keepdims=True))
            p = jnp.exp(s - m_new)
            alpha = jnp.exp(m - m_new)
            l = alpha * l + jnp.sum(p, axis=0, keepdims=True)
            return m_new, l, alpha, p.astype(BF16)

        def weigh(h, m_new, l, alpha, p):
            vtj = vt_ref[0, j, h * dv:(h + 1) * dv, :]
            acc = alpha * carry[h][2] + jnp.dot(vtj, p, preferred_element_type=F32)
            return m_new, l, acc

        scores = [score(h) for h in range(n_heads)]
        probs = [softmax(h, s) for h, s in enumerate(scores)]
        return tuple(weigh(h, *pr) for h, pr in enumerate(probs))

    def below_diagonal():
        init = tuple((jnp.full((1, 2 * tq), NEG_INF, F32), jnp.zeros((1, 2 * tq), F32),
                      jnp.zeros((dv, 2 * tq), F32)) for _ in range(n_heads))
        st["carry"] = lax.fori_loop(0, i, lambda j, c: update(j, c, False), init)

    def diagonal():
        st["carry"] = update(i, st["carry"], True)

    def finish():
        for h in range(n_heads):
            m, l, acc = st["carry"][h]
            ot = acc[:, :tq] / l[:, :tq] - lam * (acc[:, tq:] / l[:, tq:])
            o_ref[0, off:off + tq, h * dv:(h + 1) * dv] = _rms_rows(ot.T, sub_ref[...]) * (1.0 - lam_init)

    return [prepare, below_diagonal, diagonal, finish]


def _stream_sequences(stream, first_seq, q_ref, kn_ref, vn_ref, os_ref, lam, sub, lam_init, parts):
    dv = stream.dv
    for k in range(q_ref.shape[0]):
        qbd, state = stream.init_state(q_ref[k].astype(F32), kn_ref[k].astype(F32), vn_ref[k])
        for chunk in range(stream.n_chunks):
            stream.wait(chunk % stream.n_slots)
            state = stream.consume(chunk % stream.n_slots, qbd, state)
            stream.start_after(first_seq + k, chunk)
            if parts:
                parts.pop(0)()

        def store(h, o, k=k):
            os_ref[k, :, h * dv:(h + 1) * dv] = o

        stream.finalize(state, lam, sub, lam_init, store)
    while parts:
        parts.pop(0)()


def _prompt_attn_kernel(pt_ref, qt_ref, k_ref, vt_ref, lq1_ref, lk1_ref, lq2_ref, lk2_ref, sub_ref,
                        q_ref, kn_ref, vn_ref, ckt_hbm, cv_hbm, o_ref, os_ref, kbuf, vbuf, ksem, vsem,
                        *, n_heads, dqk, page, ch, n_chunks, lam_init):
    step = pl.program_id(0) * pl.num_programs(1) + pl.program_id(1)
    n_steps = pl.num_programs(0) * pl.num_programs(1)
    seqs_per_step = q_ref.shape[0]
    stream = _SampleStream(pt_ref, ckt_hbm, cv_hbm, kbuf, vbuf, ksem, vsem, n_seqs=n_steps * seqs_per_step,
                           n_heads=n_heads, dqk=dqk, dv=q_ref.shape[-1] // n_heads, page=page, ch=ch,
                           n_chunks=n_chunks, t_new=kn_ref.shape[1])

    @pl.when(step == 0)
    def _():
        stream.prologue()

    lam = _lam_value(lq1_ref, lk1_ref, lq2_ref, lk2_ref, lam_init)
    tq = vt_ref.shape[-1]
    tiles = qt_ref.shape[-1] // tq
    parts = []
    for t in range(tiles):
        parts += _prompt_attn_parts(pl.program_id(1) * tiles + t, t * tq, qt_ref, k_ref, vt_ref, sub_ref, o_ref,
                                    lam, n_heads=n_heads, dqk=dqk, lam_init=lam_init)
    _stream_sequences(stream, step * seqs_per_step, q_ref, kn_ref, vn_ref, os_ref, lam, sub_ref[...], lam_init,
                      parts)


def _stream_specs(q2, page, ch, n_slots, n_heads, dv, index):
    n_seqs, rows2, a = q2.shape
    per_seq = lambda r, n: pl.BlockSpec((n, r, a), lambda *ids: (index(*ids[:-1]), 0, 0))
    any_spec = pl.BlockSpec(memory_space=pl.ANY)
    scratch = [pltpu.VMEM((n_slots, a, ch * page), F32), pltpu.VMEM((n_slots, ch * page * n_heads, dv), F32),
               pltpu.SemaphoreType.DMA((n_slots,)), pltpu.SemaphoreType.DMA((n_slots,))]
    return per_seq, any_spec, scratch


def _prompt_attention(qtb, kb, vtb, lams, subln, page_table, q2, kn, vn, cache_kt, cache_v,
                      *, tiles, n_heads, dqk, ch, n_slots, lam_init):
    b, a, s = qtb.shape
    nk, tk = vtb.shape[1], vtb.shape[3]
    nq = s // (tk * tiles)
    n_seqs, rows2, _ = q2.shape
    t_new = rows2 // 2
    page = cache_kt.shape[-1]
    dv = cache_v.shape[-1]
    n_pages = page_table.shape[1]
    assert n_pages % ch == 0 and n_seqs % (b * nq) == 0 and s % (tk * tiles) == 0
    seqs_per_step = n_seqs // (b * nq)
    kern = functools.partial(_prompt_attn_kernel, n_heads=n_heads, dqk=dqk, page=page, ch=ch,
                             n_chunks=n_pages // ch, lam_init=lam_init)
    per_seq, any_spec, scratch = _stream_specs(q2, page, ch, n_slots, n_heads, dv, lambda bi, i: bi * nq + i)
    const = lambda x: pl.BlockSpec(x.shape, lambda bi, i, pt: (0,) * x.ndim, pipeline_mode=pl.Buffered(1))
    grid_spec = pltpu.PrefetchScalarGridSpec(
        num_scalar_prefetch=1,
        grid=(b, nq),
        in_specs=[pl.BlockSpec((1, a, tk * tiles), lambda bi, i, pt: (bi, 0, i)),
                  pl.BlockSpec((1, s, a), lambda bi, i, pt: (bi, 0, 0)),
                  pl.BlockSpec((1, nk, a, tk), lambda bi, i, pt: (bi, 0, 0, 0))]
        + [const(x) for x in lams] + [const(subln)]
        + [per_seq(rows2, seqs_per_step), per_seq(t_new, seqs_per_step), per_seq(t_new, seqs_per_step),
           any_spec, any_spec],
        out_specs=[pl.BlockSpec((1, tk * tiles, a), lambda bi, i, pt: (bi, i, 0)),
                   per_seq(t_new, seqs_per_step)],
        scratch_shapes=scratch,
    )
    return pl.pallas_call(
        kern,
        grid_spec=grid_spec,
        out_shape=[jax.ShapeDtypeStruct((b, s, a), F32), jax.ShapeDtypeStruct((n_seqs, t_new, a), F32)],
        compiler_params=pltpu.CompilerParams(dimension_semantics=("arbitrary",) * 2,
                                             vmem_limit_bytes=VMEM_LIMIT_BYTES),
        name="prompt_attn_sample",
    )(page_table, qtb, kb, vtb, *lams, subln, q2, kn, vn, cache_kt, cache_v)


def _k_page_copy(ckt_hbm, kbuf, sem, page_idx, slot, p, page):
    return pltpu.make_async_copy(ckt_hbm.at[page_idx], kbuf.at[slot, :, pl.ds(p * page, page)], sem.at[slot])


def _v_page_copy(cv_hbm, vbuf, sem, page_idx, slot, p, rows):
    return pltpu.make_async_copy(cv_hbm.at[page_idx], vbuf.at[slot, pl.ds(p * rows, rows), :], sem.at[slot])


class _SampleStream:
    def __init__(self, pt_ref, ckt_hbm, cv_hbm, kbuf, vbuf, ksem, vsem, *, n_seqs, n_heads, dqk, dv, page, ch,
                 n_chunks, t_new):
        self.n_slots = kbuf.shape[0]
        assert n_chunks % self.n_slots == 0 and n_heads % 2 == 0
        self.pt_ref, self.ckt_hbm, self.cv_hbm = pt_ref, ckt_hbm, cv_hbm
        self.kbuf, self.vbuf, self.ksem, self.vsem = kbuf, vbuf, ksem, vsem
        self.n_seqs, self.n_heads, self.dqk, self.dv = n_seqs, n_heads, dqk, dv
        self.page, self.ch, self.n_chunks, self.t_new = page, ch, n_chunks, t_new

    def start(self, seq, chunk, slot):
        for p in range(self.ch):
            pg = self.pt_ref[seq, chunk * self.ch + p]
            _k_page_copy(self.ckt_hbm, self.kbuf, self.ksem, pg, slot, p, self.page).start()
            _v_page_copy(self.cv_hbm, self.vbuf, self.vsem, pg, slot, p, self.page * self.n_heads).start()

    def wait(self, slot):
        for p in range(self.ch):
            _k_page_copy(self.ckt_hbm, self.kbuf, self.ksem, 0, slot, p, self.page).wait()
            _v_page_copy(self.cv_hbm, self.vbuf, self.vsem, 0, slot, p, self.page * self.n_heads).wait()

    def prologue(self):
        for chunk in range(self.n_slots):
            self.start(0, chunk, chunk)

    def start_after(self, seq, chunk):
        slot = chunk % self.n_slots
        nxt = chunk + self.n_slots
        if nxt < self.n_chunks:
            self.start(seq, nxt, slot)
        else:
            @pl.when(seq + 1 < self.n_seqs)
            def _():
                self.start(seq + 1, nxt - self.n_chunks, slot)

    def init_state(self, q2, kn, vn):
        t_new, dv, dqk = self.t_new, self.dv, self.dqk
        rows2 = 2 * t_new
        lane = lax.broadcasted_iota(jnp.int32, (rows2, dv), 1)
        rowi = lax.broadcasted_iota(jnp.int32, (rows2, dv), 0)
        lane_lo = jnp.where(rowi < t_new, 0, dqk)
        comp_mask = jnp.logical_and(lane >= lane_lo, lane < lane_lo + dqk)
        t_row = lax.broadcasted_iota(jnp.int32, (rows2, 1), 0) % t_new
        zero = jnp.zeros((rows2, dv), F32)
        qbd, state = [], []
        for pair in range(self.n_heads // 2):
            per_head = []
            for h in (2 * pair, 2 * pair + 1):
                qf = jnp.where(comp_mask, q2[:, h * dv:(h + 1) * dv], 0.0)
                s_new = [jnp.sum(qf * kn[j:j + 1, h * dv:(h + 1) * dv], axis=-1, keepdims=True)
                         for j in range(t_new)]
                valid = [t_row >= j for j in range(t_new)]
                m = s_new[0]
                for j in range(1, t_new):
                    m = jnp.maximum(m, jnp.where(valid[j], s_new[j], NEG_INF))
                l = jnp.zeros((rows2, 1), F32)
                acc = zero
                for j in range(t_new):
                    pj = jnp.where(valid[j], jnp.exp(s_new[j] - m), 0.0)
                    l = l + pj
                    acc = acc + pj * vn[j:j + 1, h * dv:(h + 1) * dv]
                per_head.append((qf, m, l, acc))
            (q0, m0, l0, a0), (q1, m1, l1, a1) = per_head
            qbd.append(jnp.concatenate([jnp.concatenate([q0, zero], axis=1),
                                        jnp.concatenate([zero, q1], axis=1)], axis=0))
            state.append((jnp.concatenate([m0, m1], axis=0), jnp.concatenate([l0, l1], axis=0),
                          jnp.concatenate([jnp.concatenate([a0, zero], axis=1),
                                           jnp.concatenate([zero, a1], axis=1)], axis=0)))
        return qbd, state

    def consume(self, slot, qbd, state):
        dv = self.dv
        tokens = self.ch * self.page
        out = []
        for pair, (m, l, acc) in enumerate(state):
            h0, h1 = 2 * pair, 2 * pair + 1
            kt = self.kbuf[slot, h0 * dv:(h1 + 1) * dv, :]
            s = jnp.dot(qbd[pair], kt, preferred_element_type=F32)
            m_new = jnp.maximum(m, jnp.max(s, axis=-1, keepdims=True))
            p = jnp.exp(s - m_new)
            alpha = jnp.exp(m - m_new)
            l = alpha * l + jnp.sum(p, axis=-1, keepdims=True)
            v2 = jnp.concatenate([self.vbuf[slot, pl.ds(h0, tokens, stride=self.n_heads), :],
                                  self.vbuf[slot, pl.ds(h1, tokens, stride=self.n_heads), :]], axis=1)
            acc = alpha * acc + jnp.dot(p, v2, preferred_element_type=F32)
            out.append((m_new, l, acc))
        return out

    def finalize(self, state, lam, sub, lam_init, store):
        t_new, dv = self.t_new, self.dv
        rows2 = 2 * t_new
        for pair, (m, l, acc) in enumerate(state):
            for k in range(2):
                a_h = acc[k * rows2:(k + 1) * rows2, k * dv:(k + 1) * dv]
                l_h = l[k * rows2:(k + 1) * rows2]
                o = a_h[:t_new] / l_h[:t_new] - lam * (a_h[t_new:] / l_h[t_new:])
                store(2 * pair + k, _rms_rows(o, sub) * (1.0 - lam_init))


def _finish_parts(x_ref, a_ref, c_ref, wout_ref, gmlp_ref, w1_ref, w2_ref, y_ref, ff_chunk):
    a_w = a_ref.shape[1]
    d_ff = w1_ref.shape[1]
    st = {}

    def project():
        mix = jnp.dot(a_ref[...].astype(BF16), wout_ref[0:a_w, :], preferred_element_type=F32)
        mix = mix + jnp.dot(c_ref[...].astype(BF16), wout_ref[a_w:, :], preferred_element_type=F32)
        st["acc"] = x_ref[...] + mix
        st["h"] = _rms_rows(st["acc"], gmlp_ref[...]).astype(BF16)

    def ff(lo):
        z = jnp.dot(st["h"], w1_ref[:, lo:lo + ff_chunk], preferred_element_type=F32)
        z = jnp.maximum(z, 0.0)
        z = (z * z).astype(BF16)
        st["acc"] = st["acc"] + jnp.dot(z, w2_ref[lo:lo + ff_chunk, :], preferred_element_type=F32)

    def store():
        y_ref[...] = st["acc"]

    return [project] + [functools.partial(ff, lo) for lo in range(0, d_ff, ff_chunk)] + [store]


def _finish_kernel(x_ref, a_ref, c_ref, wout_ref, gmlp_ref, w1_ref, w2_ref, y_ref, *, ff_chunk):
    for part in _finish_parts(x_ref, a_ref, c_ref, wout_ref, gmlp_ref, w1_ref, w2_ref, y_ref, ff_chunk):
        part()


def _finish_sample_kernel(pt_ref, x_ref, a_ref, c_ref, wout_ref, gmlp_ref, w1_ref, w2_ref,
                          q_ref, kn_ref, vn_ref, lq1_ref, lk1_ref, lq2_ref, lk2_ref, sub_ref, ckt_hbm, cv_hbm,
                          y_ref, os_ref, kbuf, vbuf, ksem, vsem,
                          *, ff_chunk, n_heads, dqk, page, ch, n_chunks, lam_init):
    step = pl.program_id(0)
    seqs_per_step = q_ref.shape[0]
    t_new = kn_ref.shape[1]
    dv = q_ref.shape[-1] // n_heads
    stream = _SampleStream(pt_ref, ckt_hbm, cv_hbm, kbuf, vbuf, ksem, vsem,
                           n_seqs=pl.num_programs(0) * seqs_per_step, n_heads=n_heads, dqk=dqk, dv=dv,
                           page=page, ch=ch, n_chunks=n_chunks, t_new=t_new)

    @pl.when(step == 0)
    def _():
        stream.prologue()

    parts = _finish_parts(x_ref, a_ref, c_ref, wout_ref, gmlp_ref, w1_ref, w2_ref, y_ref, ff_chunk)
    lam = _lam_value(lq1_ref, lk1_ref, lq2_ref, lk2_ref, lam_init)
    _stream_sequences(stream, step * seqs_per_step, q_ref, kn_ref, vn_ref, os_ref, lam, sub_ref[...], lam_init,
                      parts)


def _finish(x2d, a2d, c2d, wout_b, gmlp, w1_b, w2_b, *, tm, ff_chunk):
    t, d = x2d.shape
    row = lambda width: pl.BlockSpec((tm, width), lambda i: (i, 0))
    return pl.pallas_call(
        functools.partial(_finish_kernel, ff_chunk=ff_chunk),
        grid=(t // tm,),
        in_specs=[row(d), row(a2d.shape[1]), row(c2d.shape[1]), _const_spec(wout_b.shape),
                  _const_spec(gmlp.shape), _const_spec(w1_b.shape), _const_spec(w2_b.shape)],
        out_specs=row(d),
        out_shape=jax.ShapeDtypeStruct((t, d), F32),
        compiler_params=pltpu.CompilerParams(dimension_semantics=("arbitrary",),
                                             vmem_limit_bytes=VMEM_LIMIT_BYTES),
        name="finish",
    )(x2d, a2d, c2d, wout_b, gmlp, w1_b, w2_b)


def _finish_sample(x2d, a2d, c2d, wout_b, gmlp, w1_b, w2_b, page_table, q2, kn, vn, lams, subln, cache_kt, cache_v,
                   *, tm, ff_chunk, n_heads, dqk, ch, n_slots, lam_init):
    t, d = x2d.shape
    n_seqs, rows2, a = q2.shape
    t_new = rows2 // 2
    n_pool, _, page = cache_kt.shape
    dv = cache_v.shape[-1]
    n_pages = page_table.shape[1]
    n_steps = t // tm
    assert n_pages % ch == 0 and n_seqs % n_steps == 0
    n_chunks = n_pages // ch
    seqs_per_step = n_seqs // n_steps
    kern = functools.partial(_finish_sample_kernel, ff_chunk=ff_chunk, n_heads=n_heads, dqk=dqk, page=page,
                             ch=ch, n_chunks=n_chunks, lam_init=lam_init)
    row = lambda width: pl.BlockSpec((tm, width), lambda i, pt: (i, 0))
    per_seq, any_spec, scratch = _stream_specs(q2, page, ch, n_slots, n_heads, dv, lambda i: i)
    const = lambda x: pl.BlockSpec(x.shape, lambda i, pt: (0,) * x.ndim, pipeline_mode=pl.Buffered(1))
    grid_spec = pltpu.PrefetchScalarGridSpec(
        num_scalar_prefetch=1,
        grid=(n_steps,),
        in_specs=[row(d), row(a2d.shape[1]), row(c2d.shape[1]), const(wout_b), const(gmlp), const(w1_b),
                  const(w2_b), per_seq(rows2, seqs_per_step), per_seq(t_new, seqs_per_step),
                  per_seq(t_new, seqs_per_step)]
        + [const(x) for x in lams] + [const(subln), any_spec, any_spec],
        out_specs=[row(d), per_seq(t_new, seqs_per_step)],
        scratch_shapes=scratch,
    )
    return pl.pallas_call(
        kern,
        grid_spec=grid_spec,
        out_shape=[jax.ShapeDtypeStruct((t, d), F32), jax.ShapeDtypeStruct((n_seqs, t_new, a), F32)],
        compiler_params=pltpu.CompilerParams(dimension_semantics=("arbitrary",),
                                             vmem_limit_bytes=VMEM_LIMIT_BYTES),
        name="finish_sample",
    )(page_table, x2d, a2d, c2d, wout_b, gmlp, w1_b, w2_b, q2, kn, vn, *lams, subln, cache_kt, cache_v)


def kernel(x_prompt, x_sample, cache_k, cache_v, state_conv, page_table, norm_mix, w_in, q_norm, k_norm,
           lambda_q1, lambda_k1, lambda_q2, lambda_k2, subln, conv_w, w_out, norm_mlp, w_ff1, w_ff2):
    depth, n_pool, page, n_heads, _, dqk = cache_k.shape
    dv = cache_v.shape[-1]
    attn_w = n_heads * dv
    conv_width = state_conv.shape[-1]
    conv_k = conv_w.shape[1]
    batch, seq, d_model = x_prompt.shape
    dec_batch, dec_seq, _ = x_sample.shape
    q_scale = dqk ** -0.5
    n_groups = attn_w // dqk

    gmat = (jnp.kron(jnp.eye(n_groups, dtype=F32), jnp.ones((dqk, dqk), F32)) / dqk).astype(BF16)

    xp = x_prompt.reshape(batch * seq, d_model)
    xs = x_sample.reshape(dec_batch * dec_seq, d_model)
    outs = {name: [] for name in ("kp", "vp", "cp", "ks", "vs", "cs")}
    for l in range(depth):
        lam_init = _lambda_init(l)
        win_b = w_in[l].astype(BF16)
        wout_b = w_out[l].astype(BF16)
        w1_b = w_ff1[l].astype(BF16)
        w2_b = w_ff2[l].astype(BF16)
        gmix = norm_mix[l][None]
        gmlp = norm_mlp[l][None]
        gq = jnp.tile(q_norm[l], n_groups)[None]
        gk = jnp.tile(k_norm[l], n_groups)[None]
        lams = [lambda_q1[l][None], lambda_k1[l][None], lambda_q2[l][None], lambda_k2[l][None]]
        sub = subln[l][None]
        cw = conv_w[l]
        pre = dict(attn_w=attn_w, conv_w=conv_width, n_heads=n_heads, q_scale=q_scale)
        fin = functools.partial(_finish, ff_chunk=1024)

        st = state_conv[l]
        hist = jnp.stack([
            jnp.concatenate([st[:, conv_k - 1 - s:, :],
                             jnp.zeros((dec_batch, dec_seq - s, conv_width), F32)], axis=1)
            .reshape(dec_batch * dec_seq, conv_width)
            for s in range(1, conv_k)])
        ts = dec_batch * dec_seq
        qb_s, k, v, kb_s, c_s, u = _prestage_sample(xs, gmix, win_b, gq, gk, gmat, cw, hist, seq_len=dec_seq,
                                                    **pre)
        q2 = jnp.tile(qb_s.reshape(dec_batch, dec_seq, attn_w), (1, 2, 1))
        kn = kb_s.reshape(dec_batch, dec_seq, attn_w)
        vn = v.reshape(dec_batch, dec_seq, attn_w)
        cache_kt = jnp.transpose(cache_k[l], (0, 2, 3, 4, 1)).reshape(n_pool, attn_w, page)
        cache_vr = cache_v[l].reshape(n_pool, page * n_heads, dv)

        tm_p, tk_p, tm_f, tiles_a = 512, 256, 256, 2
        n_pre = batch * seq // tm_p
        n_att = batch * seq // (tk_p * tiles_a)
        assert dec_batch > n_pre + n_att
        share = [slice(0, n_pre), slice(n_pre, n_pre + n_att), slice(n_pre + n_att, dec_batch)]
        stream_in = lambda sl: (page_table[sl], q2[sl], kn[sl], vn[sl])
        stream_kw = dict(n_heads=n_heads, dqk=dqk, ch=16, lam_init=lam_init)

        wqt_b = w_in[l][:, :attn_w].T.astype(BF16)
        wkt_b = w_in[l][:, attn_w:2 * attn_w].T.astype(BF16)
        gqt = jnp.broadcast_to(gq.reshape(attn_w, 1), (attn_w, tm_p))
        gkt = jnp.broadcast_to(gk.reshape(attn_w, 1), (attn_w, tm_p))
        qtb, kt, kb, v_il, vtb, c, c_state, a_s0 = _prestage_prompt(
            xp, gmix, win_b[:, 2 * attn_w:], wqt_b, gqt, wkt_b, gkt, gmat, cw,
            *stream_in(share[0]), lams, sub, cache_kt, cache_vr,
            attn_w=attn_w, conv_w=conv_width, q_scale=q_scale, tm=tm_p, tk=tk_p, seq_len=seq, n_slots=2,
            **stream_kw)
        outs["kp"].append(jnp.transpose(kt.reshape(batch, n_heads, 2, dqk, seq), (0, 4, 1, 2, 3)))
        outs["vp"].append(v_il.reshape(batch, seq, n_heads, dv))
        outs["cp"].append(c_state)
        a_p, a_s1 = _prompt_attention(qtb, kb.reshape(batch, seq, attn_w), vtb, lams, sub,
                                      *stream_in(share[1]), cache_kt, cache_vr,
                                      tiles=tiles_a, n_slots=4, **stream_kw)
        xp, a_s2 = _finish_sample(xp, a_p.reshape(batch * seq, attn_w), c, wout_b, gmlp, w1_b, w2_b,
                                  *stream_in(share[2]), lams, sub, cache_kt, cache_vr,
                                  tm=tm_f, ff_chunk=1024, n_slots=2, **stream_kw)
        a_s = jnp.concatenate([a_s0, a_s1, a_s2], axis=0)
        xs = fin(xs, a_s.reshape(ts, attn_w), c_s, wout_b, gmlp, w1_b, w2_b, tm=256)
        outs["ks"].append(k.reshape(dec_batch, dec_seq, n_heads, 2, dqk))
        outs["vs"].append(v.reshape(dec_batch, dec_seq, n_heads, dv))
        outs["cs"].append(u.reshape(dec_batch, dec_seq, conv_width)[:, dec_seq - (conv_k - 1):, :])

    return (xp.reshape(batch, seq, d_model), xs.reshape(dec_batch, dec_seq, d_model),
            jnp.stack(outs["kp"]), jnp.stack(outs["vp"]), jnp.stack(outs["cp"]),
            jnp.stack(outs["ks"]), jnp.stack(outs["vs"]), jnp.stack(outs["cs"]))
```

```python
import functools
import math

import jax
import jax.numpy as jnp
from jax import lax
from jax.experimental import pallas as pl
from jax.experimental.pallas import tpu as pltpu

F32 = jnp.float32
BF16 = jnp.bfloat16
EPS = 1e-6
NEG_INF = -1e30

V7X_VMEM_BYTES = 64 * 1024 * 1024
V7X_SUBLANES = 8
VMEM_LIMIT_BYTES = V7X_VMEM_BYTES - 8 * 1024 * 1024


def _lambda_init(layer):
    return 0.8 - 0.6 * math.exp(-0.3 * layer)


def _rms_rows(x, g):
    ms = jnp.mean(x * x, axis=-1, keepdims=True)
    return x * lax.rsqrt(ms + EPS) * g


def _const_spec(shape):
    zeros = (0,) * len(shape)
    return pl.BlockSpec(shape, lambda *_: zeros, pipeline_mode=pl.Buffered(1))


def _lam_value(lq1_ref, lk1_ref, lq2_ref, lk2_ref, lam_init):
    s1 = jnp.sum(lq1_ref[...] * lk1_ref[...], axis=-1, keepdims=True)
    s2 = jnp.sum(lq2_ref[...] * lk2_ref[...], axis=-1, keepdims=True)
    return jnp.exp(s1) - jnp.exp(s2) + lam_init


def _prestage_parts(refs, *, attn_w, conv_w, n_heads, seq_tiles, seq_len, q_scale, sample):
    if sample:
        (x_ref, gmix_ref, win_ref, gq_ref, gk_ref, gmat_ref, cw_ref, hist_ref,
         qb_ref, k_ref, v_ref, kb_ref, c_ref, u_ref, ubuf) = refs
    else:
        (x_ref, gmix_ref, win_ref, wqt_ref, gqt_ref, wkt_ref, gkt_ref, gmat_ref, cw_ref,
         qtb_ref, kt_ref, kb_ref, vil_ref, vtb_ref, c_ref, cs_ref, ubuf) = refs
    tm = x_ref.shape[0]
    conv_k = cw_ref.shape[0]
    a, c = attn_w, conv_w
    dv = a // n_heads
    base = 0 if sample else 2 * a
    st = {}

    def proj(lo, width):
        return jnp.dot(st["h"], win_ref[:, lo - base:lo - base + width], preferred_element_type=F32)

    def group_rms(z, g):
        msq = jnp.dot((z * z).astype(BF16), gmat_ref[...], preferred_element_type=F32)
        return z * lax.rsqrt(msq + EPS) * g

    def group_rms_t(wt_ref, gt_ref):
        zt = lax.dot_general(wt_ref[...], st["h"], (((1,), (1,)), ((), ())), preferred_element_type=F32)
        msq = jnp.dot(gmat_ref[...], (zt * zt).astype(BF16), preferred_element_type=F32)
        return zt * lax.rsqrt(msq + EPS) * gt_ref[...]

    def queries():
        st["h"] = _rms_rows(x_ref[...], gmix_ref[...]).astype(BF16)
        if sample:
            qb_ref[...] = (group_rms(proj(0, a), gq_ref[...]) * q_scale).astype(BF16)
        else:
            qtb_ref[0] = (group_rms_t(wqt_ref, gqt_ref) * q_scale).astype(BF16)

    def keys():
        if sample:
            k = group_rms(proj(a, a), gk_ref[...])
            k_ref[...] = k
            kb_ref[...] = k.astype(BF16)
        else:
            kt = group_rms_t(wkt_ref, gkt_ref)
            kt_ref[0] = kt
            kb_ref[...] = kt.T.astype(BF16)

    def values():
        v = proj(2 * a, a)
        if sample:
            v_ref[...] = v
        else:
            for hh in range(n_heads):
                vil_ref[pl.ds(hh, tm, stride=n_heads), :] = v[:, hh * dv:(hh + 1) * dv]
            vt = v.T
            tk = vtb_ref.shape[-1]
            for t in range(tm // tk):
                vtb_ref[0, t] = vt[:, t * tk:(t + 1) * tk].astype(BF16)

    def conv():
        b_gate = proj(3 * a, c)
        u = proj(3 * a + c, c) * proj(3 * a + 2 * c, c)
        halo = V7X_SUBLANES
        if sample:
            ubuf[0:halo, :] = jnp.zeros((halo, c), F32)
        else:
            first = pl.program_id(0) % seq_tiles == 0

            @pl.when(first)
            def _():
                ubuf[0:halo, :] = jnp.zeros((halo, c), F32)

            @pl.when(jnp.logical_not(first))
            def _():
                ubuf[0:halo, :] = ubuf[tm:tm + halo, :]

        ubuf[halo:halo + tm, :] = u
        if sample:
            t_pos = lax.broadcasted_iota(jnp.int32, (tm, 1), 0) % seq_len
        y = None
        for j in range(conv_k):
            shift = conv_k - 1 - j
            if shift == 0:
                tap = u
            else:
                tap = ubuf[halo - shift:halo - shift + tm, :]
                if sample:
                    tap = jnp.where(t_pos >= shift, tap, hist_ref[shift - 1])
            term = cw_ref[j:j + 1, :] * tap
            y = term if y is None else y + term
        c_ref[...] = (b_gate * y).astype(c_ref.dtype)
        if sample:
            u_ref[...] = u
        else:
            cs_ref[0] = u[tm - (conv_k - 1):tm, :]

    return [queries, keys, values, conv]


def _prestage_kernel(*refs, **kw):
    for part in _prestage_parts(refs, **kw):
        part()


def _prestage_stream_kernel(pt_ref, *refs, n_in, n_out, dqk, page, ch, n_chunks, lam_init, **kw):
    pre_in, rest = refs[:n_in], refs[n_in:]
    (lq1_ref, lk1_ref, lq2_ref, lk2_ref, sub_ref, q_ref, kn_ref, vn_ref, ckt_hbm, cv_hbm), rest = rest[:10], rest[10:]
    pre_out, (os_ref, ubuf, kbuf, vbuf, ksem, vsem) = rest[:n_out], rest[n_out:]
    step = pl.program_id(0)
    seqs_per_step = q_ref.shape[0]
    n_heads = kw["n_heads"]
    stream = _SampleStream(pt_ref, ckt_hbm, cv_hbm, kbuf, vbuf, ksem, vsem,
                           n_seqs=pl.num_programs(0) * seqs_per_step, n_heads=n_heads, dqk=dqk,
                           dv=q_ref.shape[-1] // n_heads, page=page, ch=ch, n_chunks=n_chunks,
                           t_new=kn_ref.shape[1])

    @pl.when(step == 0)
    def _():
        stream.prologue()

    parts = _prestage_parts(tuple(pre_in) + tuple(pre_out) + (ubuf,), **kw)
    lam = _lam_value(lq1_ref, lk1_ref, lq2_ref, lk2_ref, lam_init)
    _stream_sequences(stream, step * seqs_per_step, q_ref, kn_ref, vn_ref, os_ref, lam, sub_ref[...], lam_init,
                      parts)


def _prestage_sample(x2d, gmix, win_b, gq, gk, gmat, cw, hist, *, attn_w, conv_w, n_heads, seq_len, q_scale):
    t, d = x2d.shape
    assert t % seq_len == 0
    args = [x2d, gmix, win_b, gq, gk, gmat, cw, hist]
    out_shape = [jax.ShapeDtypeStruct((t, attn_w), BF16), jax.ShapeDtypeStruct((t, attn_w), F32),
                 jax.ShapeDtypeStruct((t, attn_w), F32), jax.ShapeDtypeStruct((t, attn_w), BF16),
                 jax.ShapeDtypeStruct((t, conv_w), F32), jax.ShapeDtypeStruct((t, conv_w), F32)]
    kern = functools.partial(_prestage_kernel, attn_w=attn_w, conv_w=conv_w, n_heads=n_heads, seq_tiles=1,
                             seq_len=seq_len, q_scale=q_scale, sample=True)
    return pl.pallas_call(
        kern,
        grid=(1,),
        in_specs=[_const_spec(x.shape) for x in args],
        out_specs=[pl.BlockSpec(s.shape, lambda i: (0, 0)) for s in out_shape],
        out_shape=out_shape,
        scratch_shapes=[pltpu.VMEM((t + 2 * V7X_SUBLANES, conv_w), F32)],
        compiler_params=pltpu.CompilerParams(dimension_semantics=("arbitrary",),
                                             vmem_limit_bytes=VMEM_LIMIT_BYTES),
        name="prestage_sample",
    )(*args)


def _prestage_prompt(x2d, gmix, win_b, wqt_b, gqt, wkt_b, gkt, gmat, cw,
                     page_table, q2, kn, vn, lams, subln, cache_kt, cache_v,
                     *, attn_w, conv_w, n_heads, dqk, tm, tk, seq_len, q_scale, ch, n_slots, lam_init):
    t, d = x2d.shape
    conv_k = cw.shape[0]
    dv = attn_w // n_heads
    assert seq_len % tm == 0 and tm % tk == 0 and gkt.shape == (attn_w, tm) and gqt.shape == (attn_w, tm)
    seq_tiles = seq_len // tm
    n_seq = t // seq_len
    n_steps = t // tm
    n_seqs, rows2, _ = q2.shape
    t_new = rows2 // 2
    page = cache_kt.shape[-1]
    n_pages = page_table.shape[1]
    assert n_pages % ch == 0 and n_seqs % n_steps == 0
    seqs_per_step = n_seqs // n_steps
    row = lambda width: pl.BlockSpec((tm, width), lambda i, pt: (i, 0))
    col = pl.BlockSpec((1, attn_w, tm), lambda i, pt: (i // seq_tiles, 0, i % seq_tiles))
    const = lambda x: pl.BlockSpec(x.shape, lambda i, pt: (0,) * x.ndim, pipeline_mode=pl.Buffered(1))
    per_seq, any_spec, scratch = _stream_specs(q2, page, ch, n_slots, n_heads, dv, lambda i: i)
    pre_args = [x2d, gmix, win_b, wqt_b, gqt, wkt_b, gkt, gmat, cw]
    stream_args = [*lams, subln, q2, kn, vn, cache_kt, cache_v]
    in_specs = ([row(d)] + [const(x) for x in pre_args[1:]] + [const(x) for x in lams] + [const(subln)]
                + [per_seq(rows2, seqs_per_step), per_seq(t_new, seqs_per_step), per_seq(t_new, seqs_per_step),
                   any_spec, any_spec])
    out_shape = [jax.ShapeDtypeStruct((n_seq, attn_w, seq_len), BF16),
                 jax.ShapeDtypeStruct((n_seq, attn_w, seq_len), F32),
                 jax.ShapeDtypeStruct((t, attn_w), BF16),
                 jax.ShapeDtypeStruct((t * n_heads, dv), F32),
                 jax.ShapeDtypeStruct((n_seq, seq_len // tk, attn_w, tk), BF16),
                 jax.ShapeDtypeStruct((t, conv_w), BF16),
                 jax.ShapeDtypeStruct((n_seq, conv_k - 1, conv_w), F32),
                 jax.ShapeDtypeStruct((n_seqs, t_new, attn_w), F32)]
    out_specs = [col, col, row(attn_w),
                 pl.BlockSpec((tm * n_heads, dv), lambda i, pt: (i, 0)),
                 pl.BlockSpec((1, tm // tk, attn_w, tk), lambda i, pt: (i // seq_tiles, i % seq_tiles, 0, 0)),
                 row(conv_w),
                 pl.BlockSpec((1, conv_k - 1, conv_w), lambda i, pt: (i // seq_tiles, 0, 0)),
                 per_seq(t_new, seqs_per_step)]
    kern = functools.partial(_prestage_stream_kernel, n_in=len(pre_args), n_out=len(out_shape) - 1, dqk=dqk,
                             page=page, ch=ch, n_chunks=n_pages // ch, lam_init=lam_init,
                             attn_w=attn_w, conv_w=conv_w, n_heads=n_heads, seq_tiles=seq_tiles,
                             seq_len=seq_len, q_scale=q_scale, sample=False)
    grid_spec = pltpu.PrefetchScalarGridSpec(
        num_scalar_prefetch=1,
        grid=(n_steps,),
        in_specs=in_specs,
        out_specs=out_specs,
        scratch_shapes=[pltpu.VMEM((tm + 2 * V7X_SUBLANES, conv_w), F32)] + scratch,
    )
    return pl.pallas_call(
        kern,
        grid_spec=grid_spec,
        out_shape=out_shape,
        compiler_params=pltpu.CompilerParams(dimension_semantics=("arbitrary",),
                                             vmem_limit_bytes=VMEM_LIMIT_BYTES),
        name="prestage_prompt_sample",
    )(page_table, *pre_args, *stream_args)


def _prompt_attn_parts(i, off, qt_ref, k_ref, vt_ref, sub_ref, o_ref, lam, *, n_heads, dqk, lam_init):
    tq = tk = vt_ref.shape[-1]
    dv = qt_ref.shape[1] // n_heads
    st = {}

    def prepare():
        row = lax.broadcasted_iota(jnp.int32, (dv, tq), 0)
        qst = []
        for h in range(n_heads):
            qh = qt_ref[0, h * dv:(h + 1) * dv, off:off + tq]
            zero = jnp.zeros_like(qh)
            qst.append(jnp.concatenate([jnp.where(row < dqk, qh, zero), jnp.where(row >= dqk, qh, zero)],
                                       axis=1))
        st["qst"] = qst

    def update(j, carry, masked):
        qst = st["qst"]
        start = pl.multiple_of(j * tk, tk)

        def score(h):
            kj = k_ref[0, pl.ds(start, tk), h * dv:(h + 1) * dv]
            s = jnp.dot(kj, qst[h], preferred_element_type=F32)
            if masked:
                key = lax.broadcasted_iota(jnp.int32, s.shape, 0)
                qry = lax.broadcasted_iota(jnp.int32, s.shape, 1) % tq
                s = jnp.where(qry >= key, s, NEG_INF)
            return s

        def softmax(h, s):
            m, l, _ = carry[h]
            m_new = jnp.maximum(m, jnp.max(s, axis=0, keepdims=True))
            p = jnp.exp(s - m_new)
            alpha = jnp.exp(m - m_new)
            l = alpha * l + jnp.sum(p, axis=0, keepdims=True)
            return m_new, l, alpha, p.astype(BF16)

        def weigh(h, m_new, l, alpha, p):
            vtj = vt_ref[0, j, h * dv:(h + 1) * dv, :]
            acc = alpha * carry[h][2] + jnp.dot(vtj, p, preferred_element_type=F32)
            return m_new, l, acc

        scores = [score(h) for h in range(n_heads)]
        probs = [softmax(h, s) for h, s in enumerate(scores)]
        return tuple(weigh(h, *pr) for h, pr in enumerate(probs))

    def below_diagonal():
        init = tuple((jnp.full((1, 2 * tq), NEG_INF, F32), jnp.zeros((1, 2 * tq), F32),
                      jnp.zeros((dv, 2 * tq), F32)) for _ in range(n_heads))
        st["carry"] = lax.fori_loop(0, i, lambda j, c: update(j, c, False), init)

    def diagonal():
        st["carry"] = update(i, st["carry"], True)

    def finish():
        for h in range(n_heads):
            m, l, acc = st["carry"][h]
            ot = acc[:, :tq] / l[:, :tq] - lam * (acc[:, tq:] / l[:, tq:])
            o = _rms_rows(ot.T, sub_ref[...]) * (1.0 - lam_init)
            o_ref[0, off:off + tq, h * dv:(h + 1) * dv] = o.astype(o_ref.dtype)

    return [prepare, below_diagonal, diagonal, finish]


def _stream_sequences(stream, first_seq, q_ref, kn_ref, vn_ref, os_ref, lam, sub, lam_init, parts):
    dv = stream.dv
    for k in range(q_ref.shape[0]):
        qbd, state = stream.init_state(q_ref[k].astype(F32), kn_ref[k].astype(F32), vn_ref[k])
        for chunk in range(stream.n_chunks):
            stream.wait(chunk % stream.n_slots)
            state = stream.consume(chunk % stream.n_slots, qbd, state)
            stream.start_after(first_seq + k, chunk)
            if parts:
                parts.pop(0)()

        def store(h, o, k=k):
            os_ref[k, :, h * dv:(h + 1) * dv] = o

        stream.finalize(state, lam, sub, lam_init, store)
    while parts:
        parts.pop(0)()


def _prompt_attn_kernel(pt_ref, qt_ref, k_ref, vt_ref, lq1_ref, lk1_ref, lq2_ref, lk2_ref, sub_ref,
                        q_ref, kn_ref, vn_ref, ckt_hbm, cv_hbm, o_ref, os_ref, kbuf, vbuf, ksem, vsem,
                        *, n_heads, dqk, page, ch, n_chunks, lam_init):
    step = pl.program_id(0) * pl.num_programs(1) + pl.program_id(1)
    n_steps = pl.num_programs(0) * pl.num_programs(1)
    seqs_per_step = q_ref.shape[0]
    stream = _SampleStream(pt_ref, ckt_hbm, cv_hbm, kbuf, vbuf, ksem, vsem, n_seqs=n_steps * seqs_per_step,
                           n_heads=n_heads, dqk=dqk, dv=q_ref.shape[-1] // n_heads, page=page, ch=ch,
                           n_chunks=n_chunks, t_new=kn_ref.shape[1])

    @pl.when(step == 0)
    def _():
        stream.prologue()

    lam = _lam_value(lq1_ref, lk1_ref, lq2_ref, lk2_ref, lam_init)
    tq = vt_ref.shape[-1]
    tiles = qt_ref.shape[-1] // tq
    parts = []
    for t in range(tiles):
        parts += _prompt_attn_parts(pl.program_id(1) * tiles + t, t * tq, qt_ref, k_ref, vt_ref, sub_ref, o_ref,
                                    lam, n_heads=n_heads, dqk=dqk, lam_init=lam_init)
    _stream_sequences(stream, step * seqs_per_step, q_ref, kn_ref, vn_ref, os_ref, lam, sub_ref[...], lam_init,
                      parts)


def _stream_specs(q2, page, ch, n_slots, n_heads, dv, index):
    n_seqs, rows2, a = q2.shape
    per_seq = lambda r, n: pl.BlockSpec((n, r, a), lambda *ids: (index(*ids[:-1]), 0, 0))
    any_spec = pl.BlockSpec(memory_space=pl.ANY)
    scratch = [pltpu.VMEM((n_slots, a, ch * page), F32), pltpu.VMEM((n_slots, ch * page * n_heads, dv), F32),
               pltpu.SemaphoreType.DMA((n_slots,)), pltpu.SemaphoreType.DMA((n_slots,))]
    return per_seq, any_spec, scratch


def _prompt_attention(qtb, kb, vtb, lams, subln, page_table, q2, kn, vn, cache_kt, cache_v,
                      *, tiles, n_heads, dqk, ch, n_slots, lam_init):
    b, a, s = qtb.shape
    nk, tk = vtb.shape[1], vtb.shape[3]
    nq = s // (tk * tiles)
    n_seqs, rows2, _ = q2.shape
    t_new = rows2 // 2
    page = cache_kt.shape[-1]
    dv = cache_v.shape[-1]
    n_pages = page_table.shape[1]
    assert n_pages % ch == 0 and n_seqs % (b * nq) == 0 and s % (tk * tiles) == 0
    seqs_per_step = n_seqs // (b * nq)
    kern = functools.partial(_prompt_attn_kernel, n_heads=n_heads, dqk=dqk, page=page, ch=ch,
                             n_chunks=n_pages // ch, lam_init=lam_init)
    per_seq, any_spec, scratch = _stream_specs(q2, page, ch, n_slots, n_heads, dv, lambda bi, i: bi * nq + i)
    const = lambda x: pl.BlockSpec(x.shape, lambda bi, i, pt: (0,) * x.ndim, pipeline_mode=pl.Buffered(1))
    grid_spec = pltpu.PrefetchScalarGridSpec(
        num_scalar_prefetch=1,
        grid=(b, nq),
        in_specs=[pl.BlockSpec((1, a, tk * tiles), lambda bi, i, pt: (bi, 0, i)),
                  pl.BlockSpec((1, s, a), lambda bi, i, pt: (bi, 0, 0)),
                  pl.BlockSpec((1, nk, a, tk), lambda bi, i, pt: (bi, 0, 0, 0))]
        + [const(x) for x in lams] + [const(subln)]
        + [per_seq(rows2, seqs_per_step), per_seq(t_new, seqs_per_step), per_seq(t_new, seqs_per_step),
           any_spec, any_spec],
        out_specs=[pl.BlockSpec((1, tk * tiles, a), lambda bi, i, pt: (bi, i, 0)),
                   per_seq(t_new, seqs_per_step)],
        scratch_shapes=scratch,
    )
    return pl.pallas_call(
        kern,
        grid_spec=grid_spec,
        out_shape=[jax.ShapeDtypeStruct((b, s, a), BF16), jax.ShapeDtypeStruct((n_seqs, t_new, a), F32)],
        compiler_params=pltpu.CompilerParams(dimension_semantics=("arbitrary",) * 2,
                                             vmem_limit_bytes=VMEM_LIMIT_BYTES),
        name="prompt_attn_sample",
    )(page_table, qtb, kb, vtb, *lams, subln, q2, kn, vn, cache_kt, cache_v)


def _k_page_copy(ckt_hbm, kbuf, sem, page_idx, slot, p, page):
    return pltpu.make_async_copy(ckt_hbm.at[page_idx], kbuf.at[slot, :, pl.ds(p * page, page)], sem.at[slot])


def _v_page_copy(cv_hbm, vbuf, sem, page_idx, slot, p, rows):
    return pltpu.make_async_copy(cv_hbm.at[page_idx], vbuf.at[slot, pl.ds(p * rows, rows), :], sem.at[slot])


class _SampleStream:
    def __init__(self, pt_ref, ckt_hbm, cv_hbm, kbuf, vbuf, ksem, vsem, *, n_seqs, n_heads, dqk, dv, page, ch,
                 n_chunks, t_new):
        self.n_slots = kbuf.shape[0]
        assert n_chunks % self.n_slots == 0 and n_heads % 2 == 0
        self.pt_ref, self.ckt_hbm, self.cv_hbm = pt_ref, ckt_hbm, cv_hbm
        self.kbuf, self.vbuf, self.ksem, self.vsem = kbuf, vbuf, ksem, vsem
        self.n_seqs, self.n_heads, self.dqk, self.dv = n_seqs, n_heads, dqk, dv
        self.page, self.ch, self.n_chunks, self.t_new = page, ch, n_chunks, t_new

    def start(self, seq, chunk, slot):
        for p in range(self.ch):
            pg = self.pt_ref[seq, chunk * self.ch + p]
            _k_page_copy(self.ckt_hbm, self.kbuf, self.ksem, pg, slot, p, self.page).start()
            _v_page_copy(self.cv_hbm, self.vbuf, self.vsem, pg, slot, p, self.page * self.n_heads).start()

    def wait(self, slot):
        for p in range(self.ch):
            _k_page_copy(self.ckt_hbm, self.kbuf, self.ksem, 0, slot, p, self.page).wait()
            _v_page_copy(self.cv_hbm, self.vbuf, self.vsem, 0, slot, p, self.page * self.n_heads).wait()

    def prologue(self):
        for chunk in range(self.n_slots):
            self.start(0, chunk, chunk)

    def start_after(self, seq, chunk):
        slot = chunk % self.n_slots
        nxt = chunk + self.n_slots
        if nxt < self.n_chunks:
            self.start(seq, nxt, slot)
        else:
            @pl.when(seq + 1 < self.n_seqs)
            def _():
                self.start(seq + 1, nxt - self.n_chunks, slot)

    def init_state(self, q2, kn, vn):
        t_new, dv, dqk = self.t_new, self.dv, self.dqk
        rows2 = 2 * t_new
        lane = lax.broadcasted_iota(jnp.int32, (rows2, dv), 1)
        rowi = lax.broadcasted_iota(jnp.int32, (rows2, dv), 0)
        lane_lo = jnp.where(rowi < t_new, 0, dqk)
        comp_mask = jnp.logical_and(lane >= lane_lo, lane < lane_lo + dqk)
        t_row = lax.broadcasted_iota(jnp.int32, (rows2, 1), 0) % t_new
        zero = jnp.zeros((rows2, dv), F32)
        qbd, state = [], []
        for pair in range(self.n_heads // 2):
            per_head = []
            for h in (2 * pair, 2 * pair + 1):
                qf = jnp.where(comp_mask, q2[:, h * dv:(h + 1) * dv], 0.0)
                s_new = [jnp.sum(qf * kn[j:j + 1, h * dv:(h + 1) * dv], axis=-1, keepdims=True)
                         for j in range(t_new)]
                valid = [t_row >= j for j in range(t_new)]
                m = s_new[0]
                for j in range(1, t_new):
                    m = jnp.maximum(m, jnp.where(valid[j], s_new[j], NEG_INF))
                l = jnp.zeros((rows2, 1), F32)
                acc = zero
                for j in range(t_new):
                    pj = jnp.where(valid[j], jnp.exp(s_new[j] - m), 0.0)
                    l = l + pj
                    acc = acc + pj * vn[j:j + 1, h * dv:(h + 1) * dv]
                per_head.append((qf, m, l, acc))
            (q0, m0, l0, a0), (q1, m1, l1, a1) = per_head
            qbd.append(jnp.concatenate([jnp.concatenate([q0, zero], axis=1),
                                        jnp.concatenate([zero, q1], axis=1)], axis=0))
            state.append((jnp.concatenate([m0, m1], axis=0), jnp.concatenate([l0, l1], axis=0),
                          jnp.concatenate([jnp.concatenate([a0, zero], axis=1),
                                           jnp.concatenate([zero, a1], axis=1)], axis=0)))
        return qbd, state

    def consume(self, slot, qbd, state):
        dv = self.dv
        tokens = self.ch * self.page
        scores = []
        for pair in range(len(state)):
            kt = self.kbuf[slot, 2 * pair * dv:(2 * pair + 2) * dv, :]
            scores.append(jnp.dot(qbd[pair], kt, preferred_element_type=F32))
        probs = []
        for s, (m, l, _) in zip(scores, state):
            m_new = jnp.maximum(m, jnp.max(s, axis=-1, keepdims=True))
            p = jnp.exp(s - m_new)
            alpha = jnp.exp(m - m_new)
            probs.append((m_new, alpha * l + jnp.sum(p, axis=-1, keepdims=True), alpha, p))
        out = []
        for pair, (m_new, l, alpha, p) in enumerate(probs):
            h0, h1 = 2 * pair, 2 * pair + 1
            v2 = jnp.concatenate([self.vbuf[slot, pl.ds(h0, tokens, stride=self.n_heads), :],
                                  self.vbuf[slot, pl.ds(h1, tokens, stride=self.n_heads), :]], axis=1)
            acc = alpha * state[pair][2] + jnp.dot(p, v2, preferred_element_type=F32)
            out.append((m_new, l, acc))
        return out

    def finalize(self, state, lam, sub, lam_init, store):
        t_new, dv = self.t_new, self.dv
        rows2 = 2 * t_new
        for pair, (m, l, acc) in enumerate(state):
            for k in range(2):
                a_h = acc[k * rows2:(k + 1) * rows2, k * dv:(k + 1) * dv]
                l_h = l[k * rows2:(k + 1) * rows2]
                o = a_h[:t_new] / l_h[:t_new] - lam * (a_h[t_new:] / l_h[t_new:])
                store(2 * pair + k, _rms_rows(o, sub) * (1.0 - lam_init))


def _finish_parts(x_ref, a_ref, c_ref, wout_ref, gmlp_ref, w1_ref, w2_ref, y_ref, ff_chunk):
    a_w = a_ref.shape[1]
    d_ff = w1_ref.shape[1]
    st = {}

    def project():
        mix = jnp.dot(a_ref[...].astype(BF16), wout_ref[0:a_w, :], preferred_element_type=F32)
        mix = mix + jnp.dot(c_ref[...].astype(BF16), wout_ref[a_w:, :], preferred_element_type=F32)
        st["acc"] = x_ref[...] + mix
        st["h"] = _rms_rows(st["acc"], gmlp_ref[...]).astype(BF16)

    def ff(lo):
        z = jnp.dot(st["h"], w1_ref[:, lo:lo + ff_chunk], preferred_element_type=F32)
        z = jnp.maximum(z, 0.0)
        z = (z * z).astype(BF16)
        st["acc"] = st["acc"] + jnp.dot(z, w2_ref[lo:lo + ff_chunk, :], preferred_element_type=F32)

    def store():
        y_ref[...] = st["acc"]

    return [project] + [functools.partial(ff, lo) for lo in range(0, d_ff, ff_chunk)] + [store]


def _finish_kernel(x_ref, a_ref, c_ref, wout_ref, gmlp_ref, w1_ref, w2_ref, y_ref, *, ff_chunk):
    for part in _finish_parts(x_ref, a_ref, c_ref, wout_ref, gmlp_ref, w1_ref, w2_ref, y_ref, ff_chunk):
        part()


def _finish_sample_kernel(pt_ref, x_ref, a_ref, c_ref, wout_ref, gmlp_ref, w1_ref, w2_ref,
                          q_ref, kn_ref, vn_ref, lq1_ref, lk1_ref, lq2_ref, lk2_ref, sub_ref, ckt_hbm, cv_hbm,
                          y_ref, os_ref, kbuf, vbuf, ksem, vsem,
                          *, ff_chunk, n_heads, dqk, page, ch, n_chunks, lam_init):
    step = pl.program_id(0)
    seqs_per_step = q_ref.shape[0]
    t_new = kn_ref.shape[1]
    dv = q_ref.shape[-1] // n_heads
    stream = _SampleStream(pt_ref, ckt_hbm, cv_hbm, kbuf, vbuf, ksem, vsem,
                           n_seqs=pl.num_programs(0) * seqs_per_step, n_heads=n_heads, dqk=dqk, dv=dv,
                           page=page, ch=ch, n_chunks=n_chunks, t_new=t_new)

    @pl.when(step == 0)
    def _():
        stream.prologue()

    parts = _finish_parts(x_ref, a_ref, c_ref, wout_ref, gmlp_ref, w1_ref, w2_ref, y_ref, ff_chunk)
    lam = _lam_value(lq1_ref, lk1_ref, lq2_ref, lk2_ref, lam_init)
    _stream_sequences(stream, step * seqs_per_step, q_ref, kn_ref, vn_ref, os_ref, lam, sub_ref[...], lam_init,
                      parts)


def _finish(x2d, a2d, c2d, wout_b, gmlp, w1_b, w2_b, *, tm, ff_chunk):
    t, d = x2d.shape
    row = lambda width: pl.BlockSpec((tm, width), lambda i: (i, 0))
    return pl.pallas_call(
        functools.partial(_finish_kernel, ff_chunk=ff_chunk),
        grid=(t // tm,),
        in_specs=[row(d), row(a2d.shape[1]), row(c2d.shape[1]), _const_spec(wout_b.shape),
                  _const_spec(gmlp.shape), _const_spec(w1_b.shape), _const_spec(w2_b.shape)],
        out_specs=row(d),
        out_shape=jax.ShapeDtypeStruct((t, d), F32),
        compiler_params=pltpu.CompilerParams(dimension_semantics=("arbitrary",),
                                             vmem_limit_bytes=VMEM_LIMIT_BYTES),
        name="finish",
    )(x2d, a2d, c2d, wout_b, gmlp, w1_b, w2_b)


def _finish_sample(x2d, a2d, c2d, wout_b, gmlp, w1_b, w2_b, page_table, q2, kn, vn, lams, subln, cache_kt, cache_v,
                   *, tm, ff_chunk, n_heads, dqk, ch, n_slots, lam_init):
    t, d = x2d.shape
    n_seqs, rows2, a = q2.shape
    t_new = rows2 // 2
    n_pool, _, page = cache_kt.shape
    dv = cache_v.shape[-1]
    n_pages = page_table.shape[1]
    n_steps = t // tm
    assert n_pages % ch == 0 and n_seqs % n_steps == 0
    n_chunks = n_pages // ch
    seqs_per_step = n_seqs // n_steps
    kern = functools.partial(_finish_sample_kernel, ff_chunk=ff_chunk, n_heads=n_heads, dqk=dqk, page=page,
                             ch=ch, n_chunks=n_chunks, lam_init=lam_init)
    row = lambda width: pl.BlockSpec((tm, width), lambda i, pt: (i, 0))
    per_seq, any_spec, scratch = _stream_specs(q2, page, ch, n_slots, n_heads, dv, lambda i: i)
    const = lambda x: pl.BlockSpec(x.shape, lambda i, pt: (0,) * x.ndim, pipeline_mode=pl.Buffered(1))
    grid_spec = pltpu.PrefetchScalarGridSpec(
        num_scalar_prefetch=1,
        grid=(n_steps,),
        in_specs=[row(d), row(a2d.shape[1]), row(c2d.shape[1]), const(wout_b), const(gmlp), const(w1_b),
                  const(w2_b), per_seq(rows2, seqs_per_step), per_seq(t_new, seqs_per_step),
                  per_seq(t_new, seqs_per_step)]
        + [const(x) for x in lams] + [const(subln), any_spec, any_spec],
        out_specs=[row(d), per_seq(t_new, seqs_per_step)],
        scratch_shapes=scratch,
    )
    return pl.pallas_call(
        kern,
        grid_spec=grid_spec,
        out_shape=[jax.ShapeDtypeStruct((t, d), F32), jax.ShapeDtypeStruct((n_seqs, t_new, a), F32)],
        compiler_params=pltpu.CompilerParams(dimension_semantics=("arbitrary",),
                                             vmem_limit_bytes=VMEM_LIMIT_BYTES),
        name="finish_sample",
    )(page_table, x2d, a2d, c2d, wout_b, gmlp, w1_b, w2_b, q2, kn, vn, *lams, subln, cache_kt, cache_v)


def kernel(x_prompt, x_sample, cache_k, cache_v, state_conv, page_table, norm_mix, w_in, q_norm, k_norm,
           lambda_q1, lambda_k1, lambda_q2, lambda_k2, subln, conv_w, w_out, norm_mlp, w_ff1, w_ff2):
    depth, n_pool, page, n_heads, _, dqk = cache_k.shape
    dv = cache_v.shape[-1]
    attn_w = n_heads * dv
    conv_width = state_conv.shape[-1]
    conv_k = conv_w.shape[1]
    batch, seq, d_model = x_prompt.shape
    dec_batch, dec_seq, _ = x_sample.shape
    q_scale = dqk ** -0.5
    n_groups = attn_w // dqk

    gmat = (jnp.kron(jnp.eye(n_groups, dtype=F32), jnp.ones((dqk, dqk), F32)) / dqk).astype(BF16)

    xp = x_prompt.reshape(batch * seq, d_model)
    xs = x_sample.reshape(dec_batch * dec_seq, d_model)
    outs = {name: [] for name in ("kp", "vp", "cp", "ks", "vs", "cs")}
    for l in range(depth):
        lam_init = _lambda_init(l)
        win_b = w_in[l].astype(BF16)
        wout_b = w_out[l].astype(BF16)
        w1_b = w_ff1[l].astype(BF16)
        w2_b = w_ff2[l].astype(BF16)
        gmix = norm_mix[l][None]
        gmlp = norm_mlp[l][None]
        gq = jnp.tile(q_norm[l], n_groups)[None]
        gk = jnp.tile(k_norm[l], n_groups)[None]
        lams = [lambda_q1[l][None], lambda_k1[l][None], lambda_q2[l][None], lambda_k2[l][None]]
        sub = subln[l][None]
        cw = conv_w[l]
        pre = dict(attn_w=attn_w, conv_w=conv_width, n_heads=n_heads, q_scale=q_scale)
        fin = functools.partial(_finish, ff_chunk=1024)

        st = state_conv[l]
        hist = jnp.stack([
            jnp.concatenate([st[:, conv_k - 1 - s:, :],
                             jnp.zeros((dec_batch, dec_seq - s, conv_width), F32)], axis=1)
            .reshape(dec_batch * dec_seq, conv_width)
            for s in range(1, conv_k)])
        ts = dec_batch * dec_seq
        qb_s, k, v, kb_s, c_s, u = _prestage_sample(xs, gmix, win_b, gq, gk, gmat, cw, hist, seq_len=dec_seq,
                                                    **pre)
        q2 = jnp.tile(qb_s.reshape(dec_batch, dec_seq, attn_w), (1, 2, 1))
        kn = kb_s.reshape(dec_batch, dec_seq, attn_w)
        vn = v.reshape(dec_batch, dec_seq, attn_w)
        cache_kt = jnp.transpose(cache_k[l], (0, 2, 3, 4, 1)).reshape(n_pool, attn_w, page)
        cache_vr = cache_v[l].reshape(n_pool, page * n_heads, dv)

        tm_p, tk_p, tm_f, tiles_a = 512, 256, 512, 1
        n_pre = batch * seq // tm_p
        n_att = batch * seq // (tk_p * tiles_a)
        assert dec_batch > n_pre + n_att
        share = [slice(0, n_pre), slice(n_pre, n_pre + n_att), slice(n_pre + n_att, dec_batch)]
        stream_in = lambda sl: (page_table[sl], q2[sl], kn[sl], vn[sl])
        stream_kw = dict(n_heads=n_heads, dqk=dqk, ch=16, lam_init=lam_init)

        wqt_b = w_in[l][:, :attn_w].T.astype(BF16)
        wkt_b = w_in[l][:, attn_w:2 * attn_w].T.astype(BF16)
        gqt = jnp.broadcast_to(gq.reshape(attn_w, 1), (attn_w, tm_p))
        gkt = jnp.broadcast_to(gk.reshape(attn_w, 1), (attn_w, tm_p))
        qtb, kt, kb, v_il, vtb, c, c_state, a_s0 = _prestage_prompt(
            xp, gmix, win_b[:, 2 * attn_w:], wqt_b, gqt, wkt_b, gkt, gmat, cw,
            *stream_in(share[0]), lams, sub, cache_kt, cache_vr,
            attn_w=attn_w, conv_w=conv_width, q_scale=q_scale, tm=tm_p, tk=tk_p, seq_len=seq, n_slots=2,
            **stream_kw)
        outs["kp"].append(jnp.transpose(kt.reshape(batch, n_heads, 2, dqk, seq), (0, 4, 1, 2, 3)))
        outs["vp"].append(v_il.reshape(batch, seq, n_heads, dv))
        outs["cp"].append(c_state)
        a_p, a_s1 = _prompt_attention(qtb, kb.reshape(batch, seq, attn_w), vtb, lams, sub,
                                      *stream_in(share[1]), cache_kt, cache_vr,
                                      tiles=tiles_a, n_slots=4, **stream_kw)
        xp, a_s2 = _finish_sample(xp, a_p.reshape(batch * seq, attn_w), c, wout_b, gmlp, w1_b, w2_b,
                                  *stream_in(share[2]), lams, sub, cache_kt, cache_vr,
                                  tm=tm_f, ff_chunk=1024, n_slots=2, **stream_kw)
        a_s = jnp.concatenate([a_s0, a_s1, a_s2], axis=0)
        xs = fin(xs, a_s.reshape(ts, attn_w), c_s, wout_b, gmlp, w1_b, w2_b, tm=256)
        outs["ks"].append(k.reshape(dec_batch, dec_seq, n_heads, 2, dqk))
        outs["vs"].append(v.reshape(dec_batch, dec_seq, n_heads, dv))
        outs["cs"].append(u.reshape(dec_batch, dec_seq, conv_width)[:, dec_seq - (conv_k - 1):, :])

    return (xp.reshape(batch, seq, d_model), xs.reshape(dec_batch, dec_seq, d_model),
            jnp.stack(outs["kp"]), jnp.stack(outs["vp"]), jnp.stack(outs["cp"]),
            jnp.stack(outs["ks"]), jnp.stack(outs["vs"]), jnp.stack(outs["cs"]))
```

```python
import functools
import math

import jax
import jax.numpy as jnp
from jax import lax
from jax.experimental import pallas as pl
from jax.experimental.pallas import tpu as pltpu

F32 = jnp.float32
BF16 = jnp.bfloat16
EPS = 1e-6
NEG_INF = -1e30

V7X_VMEM_BYTES = 64 * 1024 * 1024
V7X_SUBLANES = 8
VMEM_LIMIT_BYTES = V7X_VMEM_BYTES - 8 * 1024 * 1024


def _lambda_init(layer):
    return 0.8 - 0.6 * math.exp(-0.3 * layer)


def _rms_rows(x, g):
    ms = jnp.mean(x * x, axis=-1, keepdims=True)
    return x * lax.rsqrt(ms + EPS) * g


def _const_spec(shape):
    zeros = (0,) * len(shape)
    return pl.BlockSpec(shape, lambda *_: zeros, pipeline_mode=pl.Buffered(1))


def _lam_value(lq1_ref, lk1_ref, lq2_ref, lk2_ref, lam_init):
    s1 = jnp.sum(lq1_ref[...] * lk1_ref[...], axis=-1, keepdims=True)
    s2 = jnp.sum(lq2_ref[...] * lk2_ref[...], axis=-1, keepdims=True)
    return jnp.exp(s1) - jnp.exp(s2) + lam_init


def _prestage_parts(refs, *, attn_w, conv_w, n_heads, seq_tiles, seq_len, q_scale, sample):
    if sample:
        (x_ref, gmix_ref, win_ref, gq_ref, gk_ref, gmat_ref, cw_ref, hist_ref,
         qb_ref, k_ref, v_ref, kb_ref, c_ref, u_ref, ubuf) = refs
    else:
        (x_ref, gmix_ref, win_ref, wqt_ref, gqt_ref, wkt_ref, gkt_ref, gmat_ref, cw_ref,
         qtb_ref, kt_ref, kb_ref, vil_ref, vtb_ref, c_ref, cs_ref, ubuf) = refs
    tm = x_ref.shape[0]
    conv_k = cw_ref.shape[0]
    a, c = attn_w, conv_w
    dv = a // n_heads
    base = 0 if sample else 2 * a
    st = {}

    def proj(lo, width):
        return jnp.dot(st["h"], win_ref[:, lo - base:lo - base + width], preferred_element_type=F32)

    def group_rms(z, g):
        msq = jnp.dot((z * z).astype(BF16), gmat_ref[...], preferred_element_type=F32)
        return z * lax.rsqrt(msq + EPS) * g

    def group_rms_t(wt_ref, gt_ref):
        zt = lax.dot_general(wt_ref[...], st["h"], (((1,), (1,)), ((), ())), preferred_element_type=F32)
        msq = jnp.dot(gmat_ref[...], (zt * zt).astype(BF16), preferred_element_type=F32)
        return zt * lax.rsqrt(msq + EPS) * gt_ref[...]

    def queries():
        st["h"] = _rms_rows(x_ref[...], gmix_ref[...]).astype(BF16)
        if sample:
            qb_ref[...] = (group_rms(proj(0, a), gq_ref[...]) * q_scale).astype(BF16)
        else:
            qtb_ref[0] = (group_rms_t(wqt_ref, gqt_ref) * q_scale).astype(BF16)

    def keys():
        if sample:
            k = group_rms(proj(a, a), gk_ref[...])
            k_ref[...] = k
            kb_ref[...] = k.astype(BF16)
        else:
            kt = group_rms_t(wkt_ref, gkt_ref)
            kt_ref[0] = kt
            kb_ref[...] = kt.T.astype(BF16)

    def values():
        v = proj(2 * a, a)
        if sample:
            v_ref[...] = v
        else:
            for hh in range(n_heads):
                vil_ref[pl.ds(hh, tm, stride=n_heads), :] = v[:, hh * dv:(hh + 1) * dv]
            vt = v.T
            tk = vtb_ref.shape[-1]
            for t in range(tm // tk):
                vtb_ref[0, t] = vt[:, t * tk:(t + 1) * tk].astype(BF16)

    def conv():
        b_gate = proj(3 * a, c)
        u = proj(3 * a + c, c) * proj(3 * a + 2 * c, c)
        halo = V7X_SUBLANES
        if sample:
            ubuf[0:halo, :] = jnp.zeros((halo, c), F32)
        else:
            first = pl.program_id(0) % seq_tiles == 0

            @pl.when(first)
            def _():
                ubuf[0:halo, :] = jnp.zeros((halo, c), F32)

            @pl.when(jnp.logical_not(first))
            def _():
                ubuf[0:halo, :] = ubuf[tm:tm + halo, :]

        ubuf[halo:halo + tm, :] = u
        if sample:
            t_pos = lax.broadcasted_iota(jnp.int32, (tm, 1), 0) % seq_len
        y = None
        for j in range(conv_k):
            shift = conv_k - 1 - j
            if shift == 0:
                tap = u
            else:
                tap = ubuf[halo - shift:halo - shift + tm, :]
                if sample:
                    tap = jnp.where(t_pos >= shift, tap, hist_ref[shift - 1])
            term = cw_ref[j:j + 1, :] * tap
            y = term if y is None else y + term
        c_ref[...] = (b_gate * y).astype(c_ref.dtype)
        if sample:
            u_ref[...] = u
        else:
            cs_ref[0] = u[tm - (conv_k - 1):tm, :]

    return [queries, keys, values, conv]


def _prestage_kernel(*refs, **kw):
    for part in _prestage_parts(refs, **kw):
        part()


def _prestage_stream_kernel(pt_ref, *refs, n_in, n_out, dqk, page, ch, n_chunks, lam_init, **kw):
    pre_in, rest = refs[:n_in], refs[n_in:]
    (lq1_ref, lk1_ref, lq2_ref, lk2_ref, sub_ref, q_ref, kn_ref, vn_ref, ckt_hbm, cv_hbm), rest = rest[:10], rest[10:]
    pre_out, (os_ref, ubuf, kbuf, vbuf, ksem, vsem) = rest[:n_out], rest[n_out:]
    step = pl.program_id(0)
    seqs_per_step = q_ref.shape[0]
    n_heads = kw["n_heads"]
    stream = _SampleStream(pt_ref, ckt_hbm, cv_hbm, kbuf, vbuf, ksem, vsem,
                           n_seqs=pl.num_programs(0) * seqs_per_step, n_heads=n_heads, dqk=dqk,
                           dv=q_ref.shape[-1] // n_heads, page=page, ch=ch, n_chunks=n_chunks,
                           t_new=kn_ref.shape[1])

    @pl.when(step == 0)
    def _():
        stream.prologue()

    parts = _prestage_parts(tuple(pre_in) + tuple(pre_out) + (ubuf,), **kw)
    lam = _lam_value(lq1_ref, lk1_ref, lq2_ref, lk2_ref, lam_init)
    _stream_sequences(stream, step * seqs_per_step, q_ref, kn_ref, vn_ref, os_ref, lam, sub_ref[...], lam_init,
                      parts)


def _prestage_sample(x2d, gmix, win_b, gq, gk, gmat, cw, hist, *, attn_w, conv_w, n_heads, seq_len, q_scale):
    t, d = x2d.shape
    assert t % seq_len == 0
    args = [x2d, gmix, win_b, gq, gk, gmat, cw, hist]
    out_shape = [jax.ShapeDtypeStruct((t, attn_w), BF16), jax.ShapeDtypeStruct((t, attn_w), F32),
                 jax.ShapeDtypeStruct((t, attn_w), F32), jax.ShapeDtypeStruct((t, attn_w), BF16),
                 jax.ShapeDtypeStruct((t, conv_w), F32), jax.ShapeDtypeStruct((t, conv_w), F32)]
    kern = functools.partial(_prestage_kernel, attn_w=attn_w, conv_w=conv_w, n_heads=n_heads, seq_tiles=1,
                             seq_len=seq_len, q_scale=q_scale, sample=True)
    return pl.pallas_call(
        kern,
        grid=(1,),
        in_specs=[_const_spec(x.shape) for x in args],
        out_specs=[pl.BlockSpec(s.shape, lambda i: (0, 0)) for s in out_shape],
        out_shape=out_shape,
        scratch_shapes=[pltpu.VMEM((t + 2 * V7X_SUBLANES, conv_w), F32)],
        compiler_params=pltpu.CompilerParams(dimension_semantics=("arbitrary",),
                                             vmem_limit_bytes=VMEM_LIMIT_BYTES),
        name="prestage_sample",
    )(*args)


def _prestage_prompt(x2d, gmix, win_b, wqt_b, gqt, wkt_b, gkt, gmat, cw,
                     page_table, q2, kn, vn, lams, subln, cache_kt, cache_v,
                     *, attn_w, conv_w, n_heads, dqk, tm, tk, seq_len, q_scale, ch, n_slots, lam_init):
    t, d = x2d.shape
    conv_k = cw.shape[0]
    dv = attn_w // n_heads
    assert seq_len % tm == 0 and tm % tk == 0 and gkt.shape == (attn_w, tm) and gqt.shape == (attn_w, tm)
    seq_tiles = seq_len // tm
    n_seq = t // seq_len
    n_steps = t // tm
    n_seqs, rows2, _ = q2.shape
    t_new = rows2 // 2
    page = cache_kt.shape[-1]
    n_pages = page_table.shape[1]
    assert n_pages % ch == 0 and n_seqs % n_steps == 0
    seqs_per_step = n_seqs // n_steps
    row = lambda width: pl.BlockSpec((tm, width), lambda i, pt: (i, 0))
    col = pl.BlockSpec((1, attn_w, tm), lambda i, pt: (i // seq_tiles, 0, i % seq_tiles))
    const = lambda x: pl.BlockSpec(x.shape, lambda i, pt: (0,) * x.ndim, pipeline_mode=pl.Buffered(1))
    per_seq, any_spec, scratch = _stream_specs(q2, page, ch, n_slots, n_heads, dv, lambda i: i)
    pre_args = [x2d, gmix, win_b, wqt_b, gqt, wkt_b, gkt, gmat, cw]
    stream_args = [*lams, subln, q2, kn, vn, cache_kt, cache_v]
    in_specs = ([row(d)] + [const(x) for x in pre_args[1:]] + [const(x) for x in lams] + [const(subln)]
                + [per_seq(rows2, seqs_per_step), per_seq(t_new, seqs_per_step), per_seq(t_new, seqs_per_step),
                   any_spec, any_spec])
    out_shape = [jax.ShapeDtypeStruct((n_seq, attn_w, seq_len), BF16),
                 jax.ShapeDtypeStruct((n_seq, attn_w, seq_len), F32),
                 jax.ShapeDtypeStruct((t, attn_w), BF16),
                 jax.ShapeDtypeStruct((t * n_heads, dv), F32),
                 jax.ShapeDtypeStruct((n_seq, seq_len // tk, attn_w, tk), BF16),
                 jax.ShapeDtypeStruct((t, conv_w), BF16),
                 jax.ShapeDtypeStruct((n_seq, conv_k - 1, conv_w), F32),
                 jax.ShapeDtypeStruct((n_seqs, t_new, attn_w), F32)]
    out_specs = [col, col, row(attn_w),
                 pl.BlockSpec((tm * n_heads, dv), lambda i, pt: (i, 0)),
                 pl.BlockSpec((1, tm // tk, attn_w, tk), lambda i, pt: (i // seq_tiles, i % seq_tiles, 0, 0)),
                 row(conv_w),
                 pl.BlockSpec((1, conv_k - 1, conv_w), lambda i, pt: (i // seq_tiles, 0, 0)),
                 per_seq(t_new, seqs_per_step)]
    kern = functools.partial(_prestage_stream_kernel, n_in=len(pre_args), n_out=len(out_shape) - 1, dqk=dqk,
                             page=page, ch=ch, n_chunks=n_pages // ch, lam_init=lam_init,
                             attn_w=attn_w, conv_w=conv_w, n_heads=n_heads, seq_tiles=seq_tiles,
                             seq_len=seq_len, q_scale=q_scale, sample=False)
    grid_spec = pltpu.PrefetchScalarGridSpec(
        num_scalar_prefetch=1,
        grid=(n_steps,),
        in_specs=in_specs,
        out_specs=out_specs,
        scratch_shapes=[pltpu.VMEM((tm + 2 * V7X_SUBLANES, conv_w), F32)] + scratch,
    )
    return pl.pallas_call(
        kern,
        grid_spec=grid_spec,
        out_shape=out_shape,
        compiler_params=pltpu.CompilerParams(dimension_semantics=("arbitrary",),
                                             vmem_limit_bytes=VMEM_LIMIT_BYTES),
        name="prestage_prompt_sample",
    )(page_table, *pre_args, *stream_args)


def _prompt_attn_parts(i, off, qt_ref, k_ref, vt_ref, sub_ref, o_ref, lam, *, n_heads, dqk, lam_init):
    tq = tk = vt_ref.shape[-1]
    dv = qt_ref.shape[1] // n_heads
    st = {}

    def prepare():
        row = lax.broadcasted_iota(jnp.int32, (dv, tq), 0)
        qst = []
        for h in range(n_heads):
            qh = qt_ref[0, h * dv:(h + 1) * dv, off:off + tq]
            zero = jnp.zeros_like(qh)
            qst.append(jnp.concatenate([jnp.where(row < dqk, qh, zero), jnp.where(row >= dqk, qh, zero)],
                                       axis=1))
        st["qst"] = qst

    def update(j, carry, masked):
        qst = st["qst"]
        start = pl.multiple_of(j * tk, tk)

        def score(h):
            kj = k_ref[0, pl.ds(start, tk), h * dv:(h + 1) * dv]
            s = jnp.dot(kj, qst[h], preferred_element_type=F32)
            if masked:
                key = lax.broadcasted_iota(jnp.int32, s.shape, 0)
                qry = lax.broadcasted_iota(jnp.int32, s.shape, 1) % tq
                s = jnp.where(qry >= key, s, NEG_INF)
            return s

        def softmax(h, s):
            m, l, _ = carry[h]
            m_new = jnp.maximum(m, jnp.max(s, axis=0, keepdims=True))
            p = jnp.exp(s - m_new)
            alpha = jnp.exp(m - m_new)
            l = alpha * l + jnp.sum(p, axis=0, keepdims=True)
            return m_new, l, alpha, p.astype(BF16)

        def weigh(h, m_new, l, alpha, p):
            vtj = vt_ref[0, j, h * dv:(h + 1) * dv, :]
            acc = alpha * carry[h][2] + jnp.dot(vtj, p, preferred_element_type=F32)
            return m_new, l, acc

        scores = [score(h) for h in range(n_heads)]
        probs = [softmax(h, s) for h, s in enumerate(scores)]
        return tuple(weigh(h, *pr) for h, pr in enumerate(probs))

    def below_diagonal():
        init = tuple((jnp.full((1, 2 * tq), NEG_INF, F32), jnp.zeros((1, 2 * tq), F32),
                      jnp.zeros((dv, 2 * tq), F32)) for _ in range(n_heads))
        st["carry"] = lax.fori_loop(0, i, lambda j, c: update(j, c, False), init)

    def diagonal():
        st["carry"] = update(i, st["carry"], True)

    def finish():
        for h in range(n_heads):
            m, l, acc = st["carry"][h]
            ot = acc[:, :tq] / l[:, :tq] - lam * (acc[:, tq:] / l[:, tq:])
            o = _rms_rows(ot.T, sub_ref[...]) * (1.0 - lam_init)
            o_ref[0, off:off + tq, h * dv:(h + 1) * dv] = o.astype(o_ref.dtype)

    return [prepare, below_diagonal, diagonal, finish]


def _stream_sequences(stream, first_seq, q_ref, kn_ref, vn_ref, os_ref, lam, sub, lam_init, parts):
    dv = stream.dv
    for k in range(q_ref.shape[0]):
        qbd, state = stream.init_state(q_ref[k].astype(F32), kn_ref[k].astype(F32), vn_ref[k])
        for chunk in range(stream.n_chunks):
            stream.wait(chunk % stream.n_slots)
            state = stream.consume(chunk % stream.n_slots, qbd, state)
            stream.start_after(first_seq + k, chunk)
            if parts:
                parts.pop(0)()

        def store(h, o, k=k):
            os_ref[k, :, h * dv:(h + 1) * dv] = o

        stream.finalize(state, lam, sub, lam_init, store)
    while parts:
        parts.pop(0)()


def _prompt_attn_kernel(pt_ref, qt_ref, k_ref, vt_ref, lq1_ref, lk1_ref, lq2_ref, lk2_ref, sub_ref,
                        q_ref, kn_ref, vn_ref, ckt_hbm, cv_hbm, o_ref, os_ref, kbuf, vbuf, ksem, vsem,
                        *, n_heads, dqk, page, ch, n_chunks, lam_init):
    step = pl.program_id(0) * pl.num_programs(1) + pl.program_id(1)
    n_steps = pl.num_programs(0) * pl.num_programs(1)
    seqs_per_step = q_ref.shape[0]
    stream = _SampleStream(pt_ref, ckt_hbm, cv_hbm, kbuf, vbuf, ksem, vsem, n_seqs=n_steps * seqs_per_step,
                           n_heads=n_heads, dqk=dqk, dv=q_ref.shape[-1] // n_heads, page=page, ch=ch,
                           n_chunks=n_chunks, t_new=kn_ref.shape[1])

    @pl.when(step == 0)
    def _():
        stream.prologue()

    lam = _lam_value(lq1_ref, lk1_ref, lq2_ref, lk2_ref, lam_init)
    tq = vt_ref.shape[-1]
    tiles = qt_ref.shape[-1] // tq
    parts = []
    for t in range(tiles):
        parts += _prompt_attn_parts(pl.program_id(1) * tiles + t, t * tq, qt_ref, k_ref, vt_ref, sub_ref, o_ref,
                                    lam, n_heads=n_heads, dqk=dqk, lam_init=lam_init)
    _stream_sequences(stream, step * seqs_per_step, q_ref, kn_ref, vn_ref, os_ref, lam, sub_ref[...], lam_init,
                      parts)


def _stream_specs(q2, page, ch, n_slots, n_heads, dv, index):
    n_seqs, rows2, a = q2.shape
    per_seq = lambda r, n: pl.BlockSpec((n, r, a), lambda *ids: (index(*ids[:-1]), 0, 0))
    any_spec = pl.BlockSpec(memory_space=pl.ANY)
    scratch = [pltpu.VMEM((n_slots, a, ch * page), F32), pltpu.VMEM((n_slots, ch * page * n_heads, dv), F32),
               pltpu.SemaphoreType.DMA((n_slots,)), pltpu.SemaphoreType.DMA((n_slots,))]
    return per_seq, any_spec, scratch


def _prompt_attention(qtb, kb, vtb, lams, subln, page_table, q2, kn, vn, cache_kt, cache_v,
                      *, tiles, n_heads, dqk, ch, n_slots, lam_init):
    b, a, s = qtb.shape
    nk, tk = vtb.shape[1], vtb.shape[3]
    nq = s // (tk * tiles)
    n_seqs, rows2, _ = q2.shape
    t_new = rows2 // 2
    page = cache_kt.shape[-1]
    dv = cache_v.shape[-1]
    n_pages = page_table.shape[1]
    assert n_pages % ch == 0 and n_seqs % (b * nq) == 0 and s % (tk * tiles) == 0
    seqs_per_step = n_seqs // (b * nq)
    kern = functools.partial(_prompt_attn_kernel, n_heads=n_heads, dqk=dqk, page=page, ch=ch,
                             n_chunks=n_pages // ch, lam_init=lam_init)
    per_seq, any_spec, scratch = _stream_specs(q2, page, ch, n_slots, n_heads, dv, lambda bi, i: bi * nq + i)
    const = lambda x: pl.BlockSpec(x.shape, lambda bi, i, pt: (0,) * x.ndim, pipeline_mode=pl.Buffered(1))
    grid_spec = pltpu.PrefetchScalarGridSpec(
        num_scalar_prefetch=1,
        grid=(b, nq),
        in_specs=[pl.BlockSpec((1, a, tk * tiles), lambda bi, i, pt: (bi, 0, i)),
                  pl.BlockSpec((1, s, a), lambda bi, i, pt: (bi, 0, 0)),
                  pl.BlockSpec((1, nk, a, tk), lambda bi, i, pt: (bi, 0, 0, 0))]
        + [const(x) for x in lams] + [const(subln)]
        + [per_seq(rows2, seqs_per_step), per_seq(t_new, seqs_per_step), per_seq(t_new, seqs_per_step),
           any_spec, any_spec],
        out_specs=[pl.BlockSpec((1, tk * tiles, a), lambda bi, i, pt: (bi, i, 0)),
                   per_seq(t_new, seqs_per_step)],
        scratch_shapes=scratch,
    )
    return pl.pallas_call(
        kern,
        grid_spec=grid_spec,
        out_shape=[jax.ShapeDtypeStruct((b, s, a), BF16), jax.ShapeDtypeStruct((n_seqs, t_new, a), F32)],
        compiler_params=pltpu.CompilerParams(dimension_semantics=("arbitrary",) * 2,
                                             vmem_limit_bytes=VMEM_LIMIT_BYTES),
        name="prompt_attn_sample",
    )(page_table, qtb, kb, vtb, *lams, subln, q2, kn, vn, cache_kt, cache_v)


def _k_page_copy(ckt_hbm, kbuf, sem, page_idx, slot, p, page):
    return pltpu.make_async_copy(ckt_hbm.at[page_idx], kbuf.at[slot, :, pl.ds(p * page, page)], sem.at[slot])


def _v_page_copy(cv_hbm, vbuf, sem, page_idx, slot, p, rows):
    return pltpu.make_async_copy(cv_hbm.at[page_idx], vbuf.at[slot, pl.ds(p * rows, rows), :], sem.at[slot])


class _SampleStream:
    def __init__(self, pt_ref, ckt_hbm, cv_hbm, kbuf, vbuf, ksem, vsem, *, n_seqs, n_heads, dqk, dv, page, ch,
                 n_chunks, t_new):
        self.n_slots = kbuf.shape[0]
        assert n_chunks % self.n_slots == 0 and n_heads % 2 == 0
        self.pt_ref, self.ckt_hbm, self.cv_hbm = pt_ref, ckt_hbm, cv_hbm
        self.kbuf, self.vbuf, self.ksem, self.vsem = kbuf, vbuf, ksem, vsem
        self.n_seqs, self.n_heads, self.dqk, self.dv = n_seqs, n_heads, dqk, dv
        self.page, self.ch, self.n_chunks, self.t_new = page, ch, n_chunks, t_new

    def start(self, seq, chunk, slot):
        for p in range(self.ch):
            pg = self.pt_ref[seq, chunk * self.ch + p]
            _k_page_copy(self.ckt_hbm, self.kbuf, self.ksem, pg, slot, p, self.page).start(priority=0)
            _v_page_copy(self.cv_hbm, self.vbuf, self.vsem, pg, slot, p, self.page * self.n_heads).start(priority=1)

    def wait(self, slot):
        for p in range(self.ch):
            _k_page_copy(self.ckt_hbm, self.kbuf, self.ksem, 0, slot, p, self.page).wait()
            _v_page_copy(self.cv_hbm, self.vbuf, self.vsem, 0, slot, p, self.page * self.n_heads).wait()

    def prologue(self):
        for chunk in range(self.n_slots):
            self.start(0, chunk, chunk)

    def start_after(self, seq, chunk):
        slot = chunk % self.n_slots
        nxt = chunk + self.n_slots
        if nxt < self.n_chunks:
            self.start(seq, nxt, slot)
        else:
            @pl.when(seq + 1 < self.n_seqs)
            def _():
                self.start(seq + 1, nxt - self.n_chunks, slot)

    def init_state(self, q2, kn, vn):
        t_new, dv, dqk = self.t_new, self.dv, self.dqk
        rows2 = 2 * t_new
        lane = lax.broadcasted_iota(jnp.int32, (rows2, dv), 1)
        rowi = lax.broadcasted_iota(jnp.int32, (rows2, dv), 0)
        lane_lo = jnp.where(rowi < t_new, 0, dqk)
        comp_mask = jnp.logical_and(lane >= lane_lo, lane < lane_lo + dqk)
        t_row = lax.broadcasted_iota(jnp.int32, (rows2, 1), 0) % t_new
        zero = jnp.zeros((rows2, dv), F32)
        qbd, state = [], []
        for pair in range(self.n_heads // 2):
            per_head = []
            for h in (2 * pair, 2 * pair + 1):
                qf = jnp.where(comp_mask, q2[:, h * dv:(h + 1) * dv], 0.0)
                s_new = [jnp.sum(qf * kn[j:j + 1, h * dv:(h + 1) * dv], axis=-1, keepdims=True)
                         for j in range(t_new)]
                valid = [t_row >= j for j in range(t_new)]
                m = s_new[0]
                for j in range(1, t_new):
                    m = jnp.maximum(m, jnp.where(valid[j], s_new[j], NEG_INF))
                l = jnp.zeros((rows2, 1), F32)
                acc = zero
                for j in range(t_new):
                    pj = jnp.where(valid[j], jnp.exp(s_new[j] - m), 0.0)
                    l = l + pj
                    acc = acc + pj * vn[j:j + 1, h * dv:(h + 1) * dv]
                per_head.append((qf, m, l, acc))
            (q0, m0, l0, a0), (q1, m1, l1, a1) = per_head
            qbd.append(jnp.concatenate([jnp.concatenate([q0, zero], axis=1),
                                        jnp.concatenate([zero, q1], axis=1)], axis=0))
            state.append((jnp.concatenate([m0, m1], axis=0), jnp.concatenate([l0, l1], axis=0),
                          jnp.concatenate([jnp.concatenate([a0, zero], axis=1),
                                           jnp.concatenate([zero, a1], axis=1)], axis=0)))
        return qbd, state

    def consume(self, slot, qbd, state):
        dv = self.dv
        tokens = self.ch * self.page
        scores = []
        for pair in range(len(state)):
            kt = self.kbuf[slot, 2 * pair * dv:(2 * pair + 2) * dv, :]
            scores.append(jnp.dot(qbd[pair], kt, preferred_element_type=F32))
        probs = []
        for s, (m, l, _) in zip(scores, state):
            m_new = jnp.maximum(m, jnp.max(s, axis=-1, keepdims=True))
            p = jnp.exp(s - m_new)
            alpha = jnp.exp(m - m_new)
            probs.append((m_new, alpha * l + jnp.sum(p, axis=-1, keepdims=True), alpha, p))
        out = []
        for pair, (m_new, l, alpha, p) in enumerate(probs):
            h0, h1 = 2 * pair, 2 * pair + 1
            v2 = jnp.concatenate([self.vbuf[slot, pl.ds(h0, tokens, stride=self.n_heads), :],
                                  self.vbuf[slot, pl.ds(h1, tokens, stride=self.n_heads), :]], axis=1)
            acc = alpha * state[pair][2] + jnp.dot(p, v2, preferred_element_type=F32)
            out.append((m_new, l, acc))
        return out

    def finalize(self, state, lam, sub, lam_init, store):
        t_new, dv = self.t_new, self.dv
        rows2 = 2 * t_new
        for pair, (m, l, acc) in enumerate(state):
            for k in range(2):
                a_h = acc[k * rows2:(k + 1) * rows2, k * dv:(k + 1) * dv]
                l_h = l[k * rows2:(k + 1) * rows2]
                o = a_h[:t_new] / l_h[:t_new] - lam * (a_h[t_new:] / l_h[t_new:])
                store(2 * pair + k, _rms_rows(o, sub) * (1.0 - lam_init))


def _finish_parts(x_ref, a_ref, c_ref, wout_ref, gmlp_ref, w1_ref, w2_ref, y_ref, ff_chunk):
    a_w = a_ref.shape[1]
    d_ff = w1_ref.shape[1]
    st = {}

    def project():
        mix = jnp.dot(a_ref[...].astype(BF16), wout_ref[0:a_w, :], preferred_element_type=F32)
        mix = mix + jnp.dot(c_ref[...].astype(BF16), wout_ref[a_w:, :], preferred_element_type=F32)
        st["acc"] = x_ref[...] + mix
        st["h"] = _rms_rows(st["acc"], gmlp_ref[...]).astype(BF16)

    def ff(lo):
        z = jnp.dot(st["h"], w1_ref[:, lo:lo + ff_chunk], preferred_element_type=F32)
        z = jnp.maximum(z, 0.0)
        z = (z * z).astype(BF16)
        st["acc"] = st["acc"] + jnp.dot(z, w2_ref[lo:lo + ff_chunk, :], preferred_element_type=F32)

    def store():
        y_ref[...] = st["acc"]

    return [project] + [functools.partial(ff, lo) for lo in range(0, d_ff, ff_chunk)] + [store]


def _finish_kernel(x_ref, a_ref, c_ref, wout_ref, gmlp_ref, w1_ref, w2_ref, y_ref, *, ff_chunk):
    for part in _finish_parts(x_ref, a_ref, c_ref, wout_ref, gmlp_ref, w1_ref, w2_ref, y_ref, ff_chunk):
        part()


def _finish_sample_kernel(pt_ref, x_ref, a_ref, c_ref, wout_ref, gmlp_ref, w1_ref, w2_ref,
                          q_ref, kn_ref, vn_ref, lq1_ref, lk1_ref, lq2_ref, lk2_ref, sub_ref, ckt_hbm, cv_hbm,
                          y_ref, os_ref, kbuf, vbuf, ksem, vsem,
                          *, ff_chunk, n_heads, dqk, page, ch, n_chunks, lam_init):
    step = pl.program_id(0)
    seqs_per_step = q_ref.shape[0]
    t_new = kn_ref.shape[1]
    dv = q_ref.shape[-1] // n_heads
    stream = _SampleStream(pt_ref, ckt_hbm, cv_hbm, kbuf, vbuf, ksem, vsem,
                           n_seqs=pl.num_programs(0) * seqs_per_step, n_heads=n_heads, dqk=dqk, dv=dv,
                           page=page, ch=ch, n_chunks=n_chunks, t_new=t_new)

    @pl.when(step == 0)
    def _():
        stream.prologue()

    parts = _finish_parts(x_ref, a_ref, c_ref, wout_ref, gmlp_ref, w1_ref, w2_ref, y_ref, ff_chunk)
    lam = _lam_value(lq1_ref, lk1_ref, lq2_ref, lk2_ref, lam_init)
    _stream_sequences(stream, step * seqs_per_step, q_ref, kn_ref, vn_ref, os_ref, lam, sub_ref[...], lam_init,
                      parts)


def _finish(x2d, a2d, c2d, wout_b, gmlp, w1_b, w2_b, *, tm, ff_chunk):
    t, d = x2d.shape
    row = lambda width: pl.BlockSpec((tm, width), lambda i: (i, 0))
    return pl.pallas_call(
        functools.partial(_finish_kernel, ff_chunk=ff_chunk),
        grid=(t // tm,),
        in_specs=[row(d), row(a2d.shape[1]), row(c2d.shape[1]), _const_spec(wout_b.shape),
                  _const_spec(gmlp.shape), _const_spec(w1_b.shape), _const_spec(w2_b.shape)],
        out_specs=row(d),
        out_shape=jax.ShapeDtypeStruct((t, d), F32),
        compiler_params=pltpu.CompilerParams(dimension_semantics=("arbitrary",),
                                             vmem_limit_bytes=VMEM_LIMIT_BYTES),
        name="finish",
    )(x2d, a2d, c2d, wout_b, gmlp, w1_b, w2_b)


def _finish_sample(x2d, a2d, c2d, wout_b, gmlp, w1_b, w2_b, page_table, q2, kn, vn, lams, subln, cache_kt, cache_v,
                   *, tm, ff_chunk, n_heads, dqk, ch, n_slots, lam_init):
    t, d = x2d.shape
    n_seqs, rows2, a = q2.shape
    t_new = rows2 // 2
    n_pool, _, page = cache_kt.shape
    dv = cache_v.shape[-1]
    n_pages = page_table.shape[1]
    n_steps = t // tm
    assert n_pages % ch == 0 and n_seqs % n_steps == 0
    n_chunks = n_pages // ch
    seqs_per_step = n_seqs // n_steps
    kern = functools.partial(_finish_sample_kernel, ff_chunk=ff_chunk, n_heads=n_heads, dqk=dqk, page=page,
                             ch=ch, n_chunks=n_chunks, lam_init=lam_init)
    row = lambda width: pl.BlockSpec((tm, width), lambda i, pt: (i, 0))
    per_seq, any_spec, scratch = _stream_specs(q2, page, ch, n_slots, n_heads, dv, lambda i: i)
    const = lambda x: pl.BlockSpec(x.shape, lambda i, pt: (0,) * x.ndim, pipeline_mode=pl.Buffered(1))
    grid_spec = pltpu.PrefetchScalarGridSpec(
        num_scalar_prefetch=1,
        grid=(n_steps,),
        in_specs=[row(d), row(a2d.shape[1]), row(c2d.shape[1]), const(wout_b), const(gmlp), const(w1_b),
                  const(w2_b), per_seq(rows2, seqs_per_step), per_seq(t_new, seqs_per_step),
                  per_seq(t_new, seqs_per_step)]
        + [const(x) for x in lams] + [const(subln), any_spec, any_spec],
        out_specs=[row(d), per_seq(t_new, seqs_per_step)],
        scratch_shapes=scratch,
    )
    return pl.pallas_call(
        kern,
        grid_spec=grid_spec,
        out_shape=[jax.ShapeDtypeStruct((t, d), F32), jax.ShapeDtypeStruct((n_seqs, t_new, a), F32)],
        compiler_params=pltpu.CompilerParams(dimension_semantics=("arbitrary",),
                                             vmem_limit_bytes=VMEM_LIMIT_BYTES),
        name="finish_sample",
    )(page_table, x2d, a2d, c2d, wout_b, gmlp, w1_b, w2_b, q2, kn, vn, *lams, subln, cache_kt, cache_v)


def kernel(x_prompt, x_sample, cache_k, cache_v, state_conv, page_table, norm_mix, w_in, q_norm, k_norm,
           lambda_q1, lambda_k1, lambda_q2, lambda_k2, subln, conv_w, w_out, norm_mlp, w_ff1, w_ff2):
    depth, n_pool, page, n_heads, _, dqk = cache_k.shape
    dv = cache_v.shape[-1]
    attn_w = n_heads * dv
    conv_width = state_conv.shape[-1]
    conv_k = conv_w.shape[1]
    batch, seq, d_model = x_prompt.shape
    dec_batch, dec_seq, _ = x_sample.shape
    q_scale = dqk ** -0.5
    n_groups = attn_w // dqk

    gmat = (jnp.kron(jnp.eye(n_groups, dtype=F32), jnp.ones((dqk, dqk), F32)) / dqk).astype(BF16)

    xp = x_prompt.reshape(batch * seq, d_model)
    xs = x_sample.reshape(dec_batch * dec_seq, d_model)
    outs = {name: [] for name in ("kp", "vp", "cp", "ks", "vs", "cs")}
    for l in range(depth):
        lam_init = _lambda_init(l)
        win_b = w_in[l].astype(BF16)
        wout_b = w_out[l].astype(BF16)
        w1_b = w_ff1[l].astype(BF16)
        w2_b = w_ff2[l].astype(BF16)
        gmix = norm_mix[l][None]
        gmlp = norm_mlp[l][None]
        gq = jnp.tile(q_norm[l], n_groups)[None]
        gk = jnp.tile(k_norm[l], n_groups)[None]
        lams = [lambda_q1[l][None], lambda_k1[l][None], lambda_q2[l][None], lambda_k2[l][None]]
        sub = subln[l][None]
        cw = conv_w[l]
        pre = dict(attn_w=attn_w, conv_w=conv_width, n_heads=n_heads, q_scale=q_scale)
        fin = functools.partial(_finish, ff_chunk=1024)

        st = state_conv[l]
        hist = jnp.stack([
            jnp.concatenate([st[:, conv_k - 1 - s:, :],
                             jnp.zeros((dec_batch, dec_seq - s, conv_width), F32)], axis=1)
            .reshape(dec_batch * dec_seq, conv_width)
            for s in range(1, conv_k)])
        ts = dec_batch * dec_seq
        qb_s, k, v, kb_s, c_s, u = _prestage_sample(xs, gmix, win_b, gq, gk, gmat, cw, hist, seq_len=dec_seq,
                                                    **pre)
        q2 = jnp.tile(qb_s.reshape(dec_batch, dec_seq, attn_w), (1, 2, 1))
        kn = kb_s.reshape(dec_batch, dec_seq, attn_w)
        vn = v.reshape(dec_batch, dec_seq, attn_w)
        cache_kt = jnp.transpose(cache_k[l], (0, 2, 3, 4, 1)).reshape(n_pool, attn_w, page)
        cache_vr = cache_v[l].reshape(n_pool, page * n_heads, dv)

        tm_p, tk_p, tm_f, tiles_a = 512, 256, 512, 1
        n_pre = batch * seq // tm_p
        n_att = batch * seq // (tk_p * tiles_a)
        assert dec_batch > n_pre + n_att
        share = [slice(0, n_pre), slice(n_pre, n_pre + n_att), slice(n_pre + n_att, dec_batch)]
        stream_in = lambda sl: (page_table[sl], q2[sl], kn[sl], vn[sl])
        stream_kw = dict(n_heads=n_heads, dqk=dqk, ch=16, lam_init=lam_init)

        wqt_b = win_b[:, :attn_w].T
        wkt_b = win_b[:, attn_w:2 * attn_w].T
        gqt = jnp.broadcast_to(gq.reshape(attn_w, 1), (attn_w, tm_p))
        gkt = jnp.broadcast_to(gk.reshape(attn_w, 1), (attn_w, tm_p))
        qtb, kt, kb, v_il, vtb, c, c_state, a_s0 = _prestage_prompt(
            xp, gmix, win_b[:, 2 * attn_w:], wqt_b, gqt, wkt_b, gkt, gmat, cw,
            *stream_in(share[0]), lams, sub, cache_kt, cache_vr,
            attn_w=attn_w, conv_w=conv_width, q_scale=q_scale, tm=tm_p, tk=tk_p, seq_len=seq, n_slots=2,
            **stream_kw)
        outs["kp"].append(jnp.transpose(kt.reshape(batch, n_heads, 2, dqk, seq), (0, 4, 1, 2, 3)))
        outs["vp"].append(v_il.reshape(batch, seq, n_heads, dv))
        outs["cp"].append(c_state)
        a_p, a_s1 = _prompt_attention(qtb, kb.reshape(batch, seq, attn_w), vtb, lams, sub,
                                      *stream_in(share[1]), cache_kt, cache_vr,
                                      tiles=tiles_a, n_slots=4, **stream_kw)
        xp, a_s2 = _finish_sample(xp, a_p.reshape(batch * seq, attn_w), c, wout_b, gmlp, w1_b, w2_b,
                                  *stream_in(share[2]), lams, sub, cache_kt, cache_vr,
                                  tm=tm_f, ff_chunk=1024, n_slots=2, **stream_kw)
        a_s = jnp.concatenate([a_s0, a_s1, a_s2], axis=0)
        xs = fin(xs, a_s.reshape(ts, attn_w), c_s, wout_b, gmlp, w1_b, w2_b, tm=256)
        outs["ks"].append(k.reshape(dec_batch, dec_seq, n_heads, 2, dqk))
        outs["vs"].append(v.reshape(dec_batch, dec_seq, n_heads, dv))
        outs["cs"].append(u.reshape(dec_batch, dec_seq, conv_width)[:, dec_seq - (conv_k - 1):, :])

    return (xp.reshape(batch, seq, d_model), xs.reshape(dec_batch, dec_seq, d_model),
            jnp.stack(outs["kp"]), jnp.stack(outs["vp"]), jnp.stack(outs["cp"]),
            jnp.stack(outs["ks"]), jnp.stack(outs["vs"]), jnp.stack(outs["cs"]))
```

```python
import functools
import math

import jax
import jax.numpy as jnp
from jax import lax
from jax.experimental import pallas as pl
from jax.experimental.pallas import tpu as pltpu

F32 = jnp.float32
BF16 = jnp.bfloat16
EPS = 1e-6
NEG_INF = -1e30

V7X_VMEM_BYTES = 64 * 1024 * 1024
V7X_SUBLANES = 8
VMEM_LIMIT_BYTES = V7X_VMEM_BYTES - 8 * 1024 * 1024


def _lambda_init(layer):
    return 0.8 - 0.6 * math.exp(-0.3 * layer)


def _rms_rows(x, g):
    ms = jnp.mean(x * x, axis=-1, keepdims=True)
    return x * lax.rsqrt(ms + EPS) * g


def _const_spec(shape):
    zeros = (0,) * len(shape)
    return pl.BlockSpec(shape, lambda *_: zeros, pipeline_mode=pl.Buffered(1))


def _lam_value(lq1_ref, lk1_ref, lq2_ref, lk2_ref, lam_init):
    s1 = jnp.sum(lq1_ref[...] * lk1_ref[...], axis=-1, keepdims=True)
    s2 = jnp.sum(lq2_ref[...] * lk2_ref[...], axis=-1, keepdims=True)
    return jnp.exp(s1) - jnp.exp(s2) + lam_init


def _prestage_parts(refs, *, attn_w, conv_w, n_heads, seq_tiles, seq_len, q_scale, sample):
    if sample:
        (x_ref, gmix_ref, win_ref, gq_ref, gk_ref, gmat_ref, cw_ref, hist_ref,
         qb_ref, k_ref, v_ref, kb_ref, c_ref, u_ref, ubuf) = refs
    else:
        (x_ref, gmix_ref, win_ref, wqkt_ref, gqt_ref, gkt_ref, gmat_ref, cw_ref,
         qtb_ref, kt_ref, kb_ref, vil_ref, vtb_ref, c_ref, cs_ref, ubuf) = refs
        wqt_ref, wkt_ref = wqkt_ref.at[0:attn_w], wqkt_ref.at[attn_w:2 * attn_w]
    tm = x_ref.shape[0]
    conv_k = cw_ref.shape[0]
    a, c = attn_w, conv_w
    dv = a // n_heads
    st = {}

    def proj(lo, width):
        return jnp.dot(st["h"], win_ref[:, lo:lo + width], preferred_element_type=F32)

    def group_rms(z, g):
        msq = jnp.dot((z * z).astype(BF16), gmat_ref[...], preferred_element_type=F32)
        return z * lax.rsqrt(msq + EPS) * g

    def group_rms_t(wt_ref, gt_ref):
        zt = lax.dot_general(wt_ref[...], st["h"], (((1,), (1,)), ((), ())), preferred_element_type=F32)
        msq = jnp.dot(gmat_ref[...], (zt * zt).astype(BF16), preferred_element_type=F32)
        return zt * lax.rsqrt(msq + EPS) * gt_ref[...]

    def queries():
        st["h"] = _rms_rows(x_ref[...], gmix_ref[...]).astype(BF16)
        if sample:
            qb_ref[...] = (group_rms(proj(0, a), gq_ref[...]) * q_scale).astype(BF16)
        else:
            qtb_ref[0] = (group_rms_t(wqt_ref, gqt_ref) * q_scale).astype(BF16)

    def keys():
        if sample:
            k = group_rms(proj(a, a), gk_ref[...])
            k_ref[...] = k
            kb_ref[...] = k.astype(BF16)
        else:
            kt = group_rms_t(wkt_ref, gkt_ref)
            kt_ref[0] = kt
            kb_ref[...] = kt.T.astype(BF16)

    def values():
        v = proj(2 * a, a)
        if sample:
            v_ref[...] = v
        else:
            for hh in range(n_heads):
                vil_ref[pl.ds(hh, tm, stride=n_heads), :] = v[:, hh * dv:(hh + 1) * dv]
            vt = v.T
            tk = vtb_ref.shape[-1]
            for t in range(tm // tk):
                vtb_ref[0, t] = vt[:, t * tk:(t + 1) * tk].astype(BF16)

    def conv():
        b_gate = proj(3 * a, c)
        u = proj(3 * a + c, c) * proj(3 * a + 2 * c, c)
        halo = V7X_SUBLANES
        if sample:
            ubuf[0:halo, :] = jnp.zeros((halo, c), F32)
        else:
            first = pl.program_id(0) % seq_tiles == 0

            @pl.when(first)
            def _():
                ubuf[0:halo, :] = jnp.zeros((halo, c), F32)

            @pl.when(jnp.logical_not(first))
            def _():
                ubuf[0:halo, :] = ubuf[tm:tm + halo, :]

        ubuf[halo:halo + tm, :] = u
        if sample:
            t_pos = lax.broadcasted_iota(jnp.int32, (tm, 1), 0) % seq_len
        y = None
        for j in range(conv_k):
            shift = conv_k - 1 - j
            if shift == 0:
                tap = u
            else:
                tap = ubuf[halo - shift:halo - shift + tm, :]
                if sample:
                    tap = jnp.where(t_pos >= shift, tap, hist_ref[shift - 1])
            term = cw_ref[j:j + 1, :] * tap
            y = term if y is None else y + term
        c_ref[...] = (b_gate * y).astype(c_ref.dtype)
        if sample:
            u_ref[...] = u
        else:
            cs_ref[0] = u[tm - (conv_k - 1):tm, :]

    return [queries, keys, values, conv]


def _prestage_kernel(*refs, **kw):
    for part in _prestage_parts(refs, **kw):
        part()


def _prestage_stream_kernel(pt_ref, *refs, n_in, n_out, dqk, page, ch, n_chunks, lam_init, **kw):
    pre_in, rest = refs[:n_in], refs[n_in:]
    (lq1_ref, lk1_ref, lq2_ref, lk2_ref, sub_ref, q_ref, kn_ref, vn_ref, ckt_hbm, cv_hbm), rest = rest[:10], rest[10:]
    pre_out, (os_ref, ubuf, kbuf, vbuf, ksem, vsem) = rest[:n_out], rest[n_out:]
    step = pl.program_id(0)
    seqs_per_step = q_ref.shape[0]
    n_heads = kw["n_heads"]
    stream = _SampleStream(pt_ref, ckt_hbm, cv_hbm, kbuf, vbuf, ksem, vsem,
                           n_seqs=pl.num_programs(0) * seqs_per_step, n_heads=n_heads, dqk=dqk,
                           dv=q_ref.shape[-1] // n_heads, page=page, ch=ch, n_chunks=n_chunks,
                           t_new=kn_ref.shape[1])

    @pl.when(step == 0)
    def _():
        stream.prologue()

    parts = _prestage_parts(tuple(pre_in) + tuple(pre_out) + (ubuf,), **kw)
    lam = _lam_value(lq1_ref, lk1_ref, lq2_ref, lk2_ref, lam_init)
    _stream_sequences(stream, step * seqs_per_step, q_ref, kn_ref, vn_ref, os_ref, lam, sub_ref[...], lam_init,
                      parts)


def _prestage_sample(x2d, gmix, win_b, gq, gk, gmat, cw, hist, *, attn_w, conv_w, n_heads, seq_len, q_scale):
    t, d = x2d.shape
    assert t % seq_len == 0
    args = [x2d, gmix, win_b, gq, gk, gmat, cw, hist]
    out_shape = [jax.ShapeDtypeStruct((t, attn_w), BF16), jax.ShapeDtypeStruct((t, attn_w), F32),
                 jax.ShapeDtypeStruct((t, attn_w), F32), jax.ShapeDtypeStruct((t, attn_w), BF16),
                 jax.ShapeDtypeStruct((t, conv_w), F32), jax.ShapeDtypeStruct((t, conv_w), F32)]
    kern = functools.partial(_prestage_kernel, attn_w=attn_w, conv_w=conv_w, n_heads=n_heads, seq_tiles=1,
                             seq_len=seq_len, q_scale=q_scale, sample=True)
    return pl.pallas_call(
        kern,
        grid=(1,),
        in_specs=[_const_spec(x.shape) for x in args],
        out_specs=[pl.BlockSpec(s.shape, lambda i: (0, 0)) for s in out_shape],
        out_shape=out_shape,
        scratch_shapes=[pltpu.VMEM((t + 2 * V7X_SUBLANES, conv_w), F32)],
        compiler_params=pltpu.CompilerParams(dimension_semantics=("arbitrary",),
                                             vmem_limit_bytes=VMEM_LIMIT_BYTES),
        name="prestage_sample",
    )(*args)


def _prestage_prompt(x2d, gmix, win_b, wqkt_b, gqt, gkt, gmat, cw,
                     page_table, q2, kn, vn, lams, subln, cache_kt, cache_v,
                     *, attn_w, conv_w, n_heads, dqk, tm, tk, seq_len, q_scale, ch, n_slots, lam_init):
    t, d = x2d.shape
    conv_k = cw.shape[0]
    dv = attn_w // n_heads
    assert seq_len % tm == 0 and tm % tk == 0 and gkt.shape == (attn_w, tm) and gqt.shape == (attn_w, tm)
    seq_tiles = seq_len // tm
    n_seq = t // seq_len
    n_steps = t // tm
    n_seqs, rows2, _ = q2.shape
    t_new = rows2 // 2
    page = cache_kt.shape[-1]
    n_pages = page_table.shape[1]
    assert n_pages % ch == 0 and n_seqs % n_steps == 0
    seqs_per_step = n_seqs // n_steps
    row = lambda width: pl.BlockSpec((tm, width), lambda i, pt: (i, 0))
    col = pl.BlockSpec((1, attn_w, tm), lambda i, pt: (i // seq_tiles, 0, i % seq_tiles))
    const = lambda x: pl.BlockSpec(x.shape, lambda i, pt: (0,) * x.ndim, pipeline_mode=pl.Buffered(1))
    per_seq, any_spec, scratch = _stream_specs(q2, page, ch, n_slots, n_heads, dv, lambda i: i)
    pre_args = [x2d, gmix, win_b, wqkt_b, gqt, gkt, gmat, cw]
    stream_args = [*lams, subln, q2, kn, vn, cache_kt, cache_v]
    in_specs = ([row(d)] + [const(x) for x in pre_args[1:]] + [const(x) for x in lams] + [const(subln)]
                + [per_seq(rows2, seqs_per_step), per_seq(t_new, seqs_per_step), per_seq(t_new, seqs_per_step),
                   any_spec, any_spec])
    out_shape = [jax.ShapeDtypeStruct((n_seq, attn_w, seq_len), BF16),
                 jax.ShapeDtypeStruct((n_seq, attn_w, seq_len), F32),
                 jax.ShapeDtypeStruct((t, attn_w), BF16),
                 jax.ShapeDtypeStruct((t * n_heads, dv), F32),
                 jax.ShapeDtypeStruct((n_seq, seq_len // tk, attn_w, tk), BF16),
                 jax.ShapeDtypeStruct((t, conv_w), BF16),
                 jax.ShapeDtypeStruct((n_seq, conv_k - 1, conv_w), F32),
                 jax.ShapeDtypeStruct((n_seqs, t_new, attn_w), F32)]
    out_specs = [col, col, row(attn_w),
                 pl.BlockSpec((tm * n_heads, dv), lambda i, pt: (i, 0)),
                 pl.BlockSpec((1, tm // tk, attn_w, tk), lambda i, pt: (i // seq_tiles, i % seq_tiles, 0, 0)),
                 row(conv_w),
                 pl.BlockSpec((1, conv_k - 1, conv_w), lambda i, pt: (i // seq_tiles, 0, 0)),
                 per_seq(t_new, seqs_per_step)]
    kern = functools.partial(_prestage_stream_kernel, n_in=len(pre_args), n_out=len(out_shape) - 1, dqk=dqk,
                             page=page, ch=ch, n_chunks=n_pages // ch, lam_init=lam_init,
                             attn_w=attn_w, conv_w=conv_w, n_heads=n_heads, seq_tiles=seq_tiles,
                             seq_len=seq_len, q_scale=q_scale, sample=False)
    grid_spec = pltpu.PrefetchScalarGridSpec(
        num_scalar_prefetch=1,
        grid=(n_steps,),
        in_specs=in_specs,
        out_specs=out_specs,
        scratch_shapes=[pltpu.VMEM((tm + 2 * V7X_SUBLANES, conv_w), F32)] + scratch,
    )
    return pl.pallas_call(
        kern,
        grid_spec=grid_spec,
        out_shape=out_shape,
        compiler_params=pltpu.CompilerParams(dimension_semantics=("arbitrary",),
                                             vmem_limit_bytes=VMEM_LIMIT_BYTES),
        name="prestage_prompt_sample",
    )(page_table, *pre_args, *stream_args)


def _prompt_attn_parts(i, off, qt_ref, k_ref, vt_ref, sub_ref, o_ref, lam, *, n_heads, dqk, lam_init):
    tq = tk = vt_ref.shape[-1]
    dv = qt_ref.shape[1] // n_heads
    st = {}

    def prepare():
        row = lax.broadcasted_iota(jnp.int32, (dv, tq), 0)
        qst = []
        for h in range(n_heads):
            qh = qt_ref[0, h * dv:(h + 1) * dv, off:off + tq]
            zero = jnp.zeros_like(qh)
            qst.append(jnp.concatenate([jnp.where(row < dqk, qh, zero), jnp.where(row >= dqk, qh, zero)],
                                       axis=1))
        st["qst"] = qst

    def update(j, carry, masked):
        qst = st["qst"]
        start = pl.multiple_of(j * tk, tk)

        def score(h):
            kj = k_ref[0, pl.ds(start, tk), h * dv:(h + 1) * dv]
            s = jnp.dot(kj, qst[h], preferred_element_type=F32)
            if masked:
                key = lax.broadcasted_iota(jnp.int32, s.shape, 0)
                qry = lax.broadcasted_iota(jnp.int32, s.shape, 1) % tq
                s = jnp.where(qry >= key, s, NEG_INF)
            return s

        def softmax(h, s):
            m, l, _ = carry[h]
            m_new = jnp.maximum(m, jnp.max(s, axis=0, keepdims=True))
            p = jnp.exp(s - m_new)
            alpha = jnp.exp(m - m_new)
            l = alpha * l + jnp.sum(p, axis=0, keepdims=True)
            return m_new, l, alpha, p.astype(BF16)

        def weigh(h, m_new, l, alpha, p):
            vtj = vt_ref[0, j, h * dv:(h + 1) * dv, :]
            acc = alpha * carry[h][2] + jnp.dot(vtj, p, preferred_element_type=F32)
            return m_new, l, acc

        scores = [score(h) for h in range(n_heads)]
        probs = [softmax(h, s) for h, s in enumerate(scores)]
        return tuple(weigh(h, *pr) for h, pr in enumerate(probs))

    def below_diagonal():
        init = tuple((jnp.full((1, 2 * tq), NEG_INF, F32), jnp.zeros((1, 2 * tq), F32),
                      jnp.zeros((dv, 2 * tq), F32)) for _ in range(n_heads))
        st["carry"] = lax.fori_loop(0, i, lambda j, c: update(j, c, False), init)

    def diagonal():
        st["carry"] = update(i, st["carry"], True)

    def finish():
        for h in range(n_heads):
            m, l, acc = st["carry"][h]
            ot = acc[:, :tq] / l[:, :tq] - lam * (acc[:, tq:] / l[:, tq:])
            o = _rms_rows(ot.T, sub_ref[...]) * (1.0 - lam_init)
            o_ref[0, off:off + tq, h * dv:(h + 1) * dv] = o.astype(o_ref.dtype)

    return [prepare, below_diagonal, diagonal, finish]


def _stream_sequences(stream, first_seq, q_ref, kn_ref, vn_ref, os_ref, lam, sub, lam_init, parts):
    dv = stream.dv
    for k in range(q_ref.shape[0]):
        qbd, state = stream.init_state(q_ref[k].astype(F32), kn_ref[k].astype(F32), vn_ref[k])
        for chunk in range(stream.n_chunks):
            stream.wait(chunk % stream.n_slots)
            state = stream.consume(chunk % stream.n_slots, qbd, state)
            stream.start_after(first_seq + k, chunk)
            if parts:
                parts.pop(0)()

        def store(h, o, k=k):
            os_ref[k, :, h * dv:(h + 1) * dv] = o

        stream.finalize(state, lam, sub, lam_init, store)
    while parts:
        parts.pop(0)()


def _prompt_attn_kernel(pt_ref, qt_ref, k_ref, vt_ref, lq1_ref, lk1_ref, lq2_ref, lk2_ref, sub_ref,
                        q_ref, kn_ref, vn_ref, ckt_hbm, cv_hbm, o_ref, os_ref, kbuf, vbuf, ksem, vsem,
                        *, n_heads, dqk, page, ch, n_chunks, lam_init):
    step = pl.program_id(0) * pl.num_programs(1) + pl.program_id(1)
    n_steps = pl.num_programs(0) * pl.num_programs(1)
    seqs_per_step = q_ref.shape[0]
    stream = _SampleStream(pt_ref, ckt_hbm, cv_hbm, kbuf, vbuf, ksem, vsem, n_seqs=n_steps * seqs_per_step,
                           n_heads=n_heads, dqk=dqk, dv=q_ref.shape[-1] // n_heads, page=page, ch=ch,
                           n_chunks=n_chunks, t_new=kn_ref.shape[1])

    @pl.when(step == 0)
    def _():
        stream.prologue()

    lam = _lam_value(lq1_ref, lk1_ref, lq2_ref, lk2_ref, lam_init)
    tq = vt_ref.shape[-1]
    tiles = qt_ref.shape[-1] // tq
    parts = []
    for t in range(tiles):
        parts += _prompt_attn_parts(pl.program_id(1) * tiles + t, t * tq, qt_ref, k_ref, vt_ref, sub_ref, o_ref,
                                    lam, n_heads=n_heads, dqk=dqk, lam_init=lam_init)
    _stream_sequences(stream, step * seqs_per_step, q_ref, kn_ref, vn_ref, os_ref, lam, sub_ref[...], lam_init,
                      parts)


def _stream_specs(q2, page, ch, n_slots, n_heads, dv, index):
    n_seqs, rows2, a = q2.shape
    per_seq = lambda r, n: pl.BlockSpec((n, r, a), lambda *ids: (index(*ids[:-1]), 0, 0))
    any_spec = pl.BlockSpec(memory_space=pl.ANY)
    scratch = [pltpu.VMEM((n_slots, ch, a, page), F32), pltpu.VMEM((n_slots, ch * page * n_heads, dv), F32),
               pltpu.SemaphoreType.DMA((n_slots,)), pltpu.SemaphoreType.DMA((n_slots,))]
    return per_seq, any_spec, scratch


def _prompt_attention(qtb, kb, vtb, lams, subln, page_table, q2, kn, vn, cache_kt, cache_v,
                      *, tiles, n_heads, dqk, ch, n_slots, lam_init):
    b, a, s = qtb.shape
    nk, tk = vtb.shape[1], vtb.shape[3]
    nq = s // (tk * tiles)
    n_seqs, rows2, _ = q2.shape
    t_new = rows2 // 2
    page = cache_kt.shape[-1]
    dv = cache_v.shape[-1]
    n_pages = page_table.shape[1]
    assert n_pages % ch == 0 and n_seqs % (b * nq) == 0 and s % (tk * tiles) == 0
    seqs_per_step = n_seqs // (b * nq)
    kern = functools.partial(_prompt_attn_kernel, n_heads=n_heads, dqk=dqk, page=page, ch=ch,
                             n_chunks=n_pages // ch, lam_init=lam_init)
    per_seq, any_spec, scratch = _stream_specs(q2, page, ch, n_slots, n_heads, dv, lambda bi, i: bi * nq + i)
    const = lambda x: pl.BlockSpec(x.shape, lambda bi, i, pt: (0,) * x.ndim, pipeline_mode=pl.Buffered(1))
    grid_spec = pltpu.PrefetchScalarGridSpec(
        num_scalar_prefetch=1,
        grid=(b, nq),
        in_specs=[pl.BlockSpec((1, a, tk * tiles), lambda bi, i, pt: (bi, 0, i)),
                  pl.BlockSpec((1, s, a), lambda bi, i, pt: (bi, 0, 0)),
                  pl.BlockSpec((1, nk, a, tk), lambda bi, i, pt: (bi, 0, 0, 0))]
        + [const(x) for x in lams] + [const(subln)]
        + [per_seq(rows2, seqs_per_step), per_seq(t_new, seqs_per_step), per_seq(t_new, seqs_per_step),
           any_spec, any_spec],
        out_specs=[pl.BlockSpec((1, tk * tiles, a), lambda bi, i, pt: (bi, i, 0)),
                   per_seq(t_new, seqs_per_step)],
        scratch_shapes=scratch,
    )
    return pl.pallas_call(
        kern,
        grid_spec=grid_spec,
        out_shape=[jax.ShapeDtypeStruct((b, s, a), BF16), jax.ShapeDtypeStruct((n_seqs, t_new, a), F32)],
        compiler_params=pltpu.CompilerParams(dimension_semantics=("arbitrary",) * 2,
                                             vmem_limit_bytes=VMEM_LIMIT_BYTES),
        name="prompt_attn_sample",
    )(page_table, qtb, kb, vtb, *lams, subln, q2, kn, vn, cache_kt, cache_v)


def _k_page_copy(ckt_hbm, kbuf, sem, page_idx, slot, p):
    return pltpu.make_async_copy(ckt_hbm.at[page_idx], kbuf.at[slot, p], sem.at[slot])


def _v_page_copy(cv_hbm, vbuf, sem, page_idx, slot, p, rows):
    return pltpu.make_async_copy(cv_hbm.at[page_idx], vbuf.at[slot, pl.ds(p * rows, rows), :], sem.at[slot])


class _SampleStream:
    def __init__(self, pt_ref, ckt_hbm, cv_hbm, kbuf, vbuf, ksem, vsem, *, n_seqs, n_heads, dqk, dv, page, ch,
                 n_chunks, t_new):
        self.n_slots = kbuf.shape[0]
        assert n_chunks % self.n_slots == 0 and n_heads % 2 == 0
        self.pt_ref, self.ckt_hbm, self.cv_hbm = pt_ref, ckt_hbm, cv_hbm
        self.kbuf, self.vbuf, self.ksem, self.vsem = kbuf, vbuf, ksem, vsem
        self.n_seqs, self.n_heads, self.dqk, self.dv = n_seqs, n_heads, dqk, dv
        self.page, self.ch, self.n_chunks, self.t_new = page, ch, n_chunks, t_new

    def start(self, seq, chunk, slot):
        for p in range(self.ch):
            pg = self.pt_ref[seq, chunk * self.ch + p]
            _k_page_copy(self.ckt_hbm, self.kbuf, self.ksem, pg, slot, p).start()
            _v_page_copy(self.cv_hbm, self.vbuf, self.vsem, pg, slot, p, self.page * self.n_heads).start()

    def wait(self, slot):
        for p in range(self.ch):
            _k_page_copy(self.ckt_hbm, self.kbuf, self.ksem, 0, slot, p).wait()
            _v_page_copy(self.cv_hbm, self.vbuf, self.vsem, 0, slot, p, self.page * self.n_heads).wait()

    def prologue(self):
        for chunk in range(self.n_slots):
            self.start(0, chunk, chunk)

    def start_after(self, seq, chunk):
        slot = chunk % self.n_slots
        nxt = chunk + self.n_slots
        if nxt < self.n_chunks:
            self.start(seq, nxt, slot)
        else:
            @pl.when(seq + 1 < self.n_seqs)
            def _():
                self.start(seq + 1, nxt - self.n_chunks, slot)

    def init_state(self, q2, kn, vn):
        t_new, dv, dqk = self.t_new, self.dv, self.dqk
        rows2 = 2 * t_new
        lane = lax.broadcasted_iota(jnp.int32, (rows2, dv), 1)
        rowi = lax.broadcasted_iota(jnp.int32, (rows2, dv), 0)
        lane_lo = jnp.where(rowi < t_new, 0, dqk)
        comp_mask = jnp.logical_and(lane >= lane_lo, lane < lane_lo + dqk)
        t_row = lax.broadcasted_iota(jnp.int32, (rows2, 1), 0) % t_new
        zero = jnp.zeros((rows2, dv), F32)
        qbd, state = [], []
        for pair in range(self.n_heads // 2):
            per_head = []
            for h in (2 * pair, 2 * pair + 1):
                qf = jnp.where(comp_mask, q2[:, h * dv:(h + 1) * dv], 0.0)
                s_new = [jnp.sum(qf * kn[j:j + 1, h * dv:(h + 1) * dv], axis=-1, keepdims=True)
                         for j in range(t_new)]
                valid = [t_row >= j for j in range(t_new)]
                m = s_new[0]
                for j in range(1, t_new):
                    m = jnp.maximum(m, jnp.where(valid[j], s_new[j], NEG_INF))
                l = jnp.zeros((rows2, 1), F32)
                acc = zero
                for j in range(t_new):
                    pj = jnp.where(valid[j], jnp.exp(s_new[j] - m), 0.0)
                    l = l + pj
                    acc = acc + pj * vn[j:j + 1, h * dv:(h + 1) * dv]
                per_head.append((qf, m, l, acc))
            (q0, m0, l0, a0), (q1, m1, l1, a1) = per_head
            qbd.append(jnp.concatenate([jnp.concatenate([q0, zero], axis=1),
                                        jnp.concatenate([zero, q1], axis=1)], axis=0))
            state.append((jnp.concatenate([m0, m1], axis=0), jnp.concatenate([l0, l1], axis=0),
                          jnp.concatenate([jnp.concatenate([a0, zero], axis=1),
                                           jnp.concatenate([zero, a1], axis=1)], axis=0)))
        return qbd, state

    def consume(self, slot, qbd, state):
        dv = self.dv
        tokens = self.ch * self.page
        scores = []
        for pair in range(len(state)):
            kt = jnp.concatenate([self.kbuf[slot, p, 2 * pair * dv:(2 * pair + 2) * dv, :]
                                  for p in range(self.ch)], axis=1)
            scores.append(jnp.dot(qbd[pair], kt, preferred_element_type=F32))
        probs = []
        for s, (m, l, _) in zip(scores, state):
            m_new = jnp.maximum(m, jnp.max(s, axis=-1, keepdims=True))
            p = jnp.exp(s - m_new)
            alpha = jnp.exp(m - m_new)
            probs.append((m_new, alpha * l + jnp.sum(p, axis=-1, keepdims=True), alpha, p))
        out = []
        for pair, (m_new, l, alpha, p) in enumerate(probs):
            h0, h1 = 2 * pair, 2 * pair + 1
            v2 = jnp.concatenate([self.vbuf[slot, pl.ds(h0, tokens, stride=self.n_heads), :],
                                  self.vbuf[slot, pl.ds(h1, tokens, stride=self.n_heads), :]], axis=1)
            acc = alpha * state[pair][2] + jnp.dot(p, v2, preferred_element_type=F32)
            out.append((m_new, l, acc))
        return out

    def finalize(self, state, lam, sub, lam_init, store):
        t_new, dv = self.t_new, self.dv
        rows2 = 2 * t_new
        for pair, (m, l, acc) in enumerate(state):
            for k in range(2):
                a_h = acc[k * rows2:(k + 1) * rows2, k * dv:(k + 1) * dv]
                l_h = l[k * rows2:(k + 1) * rows2]
                o = a_h[:t_new] / l_h[:t_new] - lam * (a_h[t_new:] / l_h[t_new:])
                store(2 * pair + k, _rms_rows(o, sub) * (1.0 - lam_init))


def _finish_parts(x_ref, a_ref, c_ref, wout_ref, gmlp_ref, w1_ref, w2_ref, y_ref, ff_chunk):
    a_w = a_ref.shape[1]
    d_ff = w1_ref.shape[1]
    st = {}

    def project():
        mix = jnp.dot(a_ref[...].astype(BF16), wout_ref[0:a_w, :], preferred_element_type=F32)
        mix = mix + jnp.dot(c_ref[...].astype(BF16), wout_ref[a_w:, :], preferred_element_type=F32)
        st["acc"] = x_ref[...] + mix
        st["h"] = _rms_rows(st["acc"], gmlp_ref[...]).astype(BF16)

    def ff(lo):
        z = jnp.dot(st["h"], w1_ref[:, lo:lo + ff_chunk], preferred_element_type=F32)
        z = jnp.maximum(z, 0.0)
        z = (z * z).astype(BF16)
        st["acc"] = st["acc"] + jnp.dot(z, w2_ref[lo:lo + ff_chunk, :], preferred_element_type=F32)

    def store():
        y_ref[...] = st["acc"]

    return [project] + [functools.partial(ff, lo) for lo in range(0, d_ff, ff_chunk)] + [store]


def _finish_kernel(x_ref, a_ref, c_ref, wout_ref, gmlp_ref, w1_ref, w2_ref, y_ref, *, ff_chunk):
    for part in _finish_parts(x_ref, a_ref, c_ref, wout_ref, gmlp_ref, w1_ref, w2_ref, y_ref, ff_chunk):
        part()


def _finish_sample_kernel(pt_ref, x_ref, a_ref, c_ref, wout_ref, gmlp_ref, w1_ref, w2_ref,
                          q_ref, kn_ref, vn_ref, lq1_ref, lk1_ref, lq2_ref, lk2_ref, sub_ref, ckt_hbm, cv_hbm,
                          y_ref, os_ref, kbuf, vbuf, ksem, vsem,
                          *, ff_chunk, n_heads, dqk, page, ch, n_chunks, lam_init):
    step = pl.program_id(0)
    seqs_per_step = q_ref.shape[0]
    t_new = kn_ref.shape[1]
    dv = q_ref.shape[-1] // n_heads
    stream = _SampleStream(pt_ref, ckt_hbm, cv_hbm, kbuf, vbuf, ksem, vsem,
                           n_seqs=pl.num_programs(0) * seqs_per_step, n_heads=n_heads, dqk=dqk, dv=dv,
                           page=page, ch=ch, n_chunks=n_chunks, t_new=t_new)

    @pl.when(step == 0)
    def _():
        stream.prologue()

    parts = _finish_parts(x_ref, a_ref, c_ref, wout_ref, gmlp_ref, w1_ref, w2_ref, y_ref, ff_chunk)
    lam = _lam_value(lq1_ref, lk1_ref, lq2_ref, lk2_ref, lam_init)
    _stream_sequences(stream, step * seqs_per_step, q_ref, kn_ref, vn_ref, os_ref, lam, sub_ref[...], lam_init,
                      parts)


def _finish(x2d, a2d, c2d, wout_b, gmlp, w1_b, w2_b, *, tm, ff_chunk):
    t, d = x2d.shape
    row = lambda width: pl.BlockSpec((tm, width), lambda i: (i, 0))
    return pl.pallas_call(
        functools.partial(_finish_kernel, ff_chunk=ff_chunk),
        grid=(t // tm,),
        in_specs=[row(d), row(a2d.shape[1]), row(c2d.shape[1]), _const_spec(wout_b.shape),
                  _const_spec(gmlp.shape), _const_spec(w1_b.shape), _const_spec(w2_b.shape)],
        out_specs=row(d),
        out_shape=jax.ShapeDtypeStruct((t, d), F32),
        compiler_params=pltpu.CompilerParams(dimension_semantics=("arbitrary",),
                                             vmem_limit_bytes=VMEM_LIMIT_BYTES),
        name="finish",
    )(x2d, a2d, c2d, wout_b, gmlp, w1_b, w2_b)


def _finish_sample(x2d, a2d, c2d, wout_b, gmlp, w1_b, w2_b, page_table, q2, kn, vn, lams, subln, cache_kt, cache_v,
                   *, tm, ff_chunk, n_heads, dqk, ch, n_slots, lam_init):
    t, d = x2d.shape
    n_seqs, rows2, a = q2.shape
    t_new = rows2 // 2
    n_pool, _, page = cache_kt.shape
    dv = cache_v.shape[-1]
    n_pages = page_table.shape[1]
    n_steps = t // tm
    assert n_pages % ch == 0 and n_seqs % n_steps == 0
    n_chunks = n_pages // ch
    seqs_per_step = n_seqs // n_steps
    kern = functools.partial(_finish_sample_kernel, ff_chunk=ff_chunk, n_heads=n_heads, dqk=dqk, page=page,
                             ch=ch, n_chunks=n_chunks, lam_init=lam_init)
    row = lambda width: pl.BlockSpec((tm, width), lambda i, pt: (i, 0))
    per_seq, any_spec, scratch = _stream_specs(q2, page, ch, n_slots, n_heads, dv, lambda i: i)
    const = lambda x: pl.BlockSpec(x.shape, lambda i, pt: (0,) * x.ndim, pipeline_mode=pl.Buffered(1))
    grid_spec = pltpu.PrefetchScalarGridSpec(
        num_scalar_prefetch=1,
        grid=(n_steps,),
        in_specs=[row(d), row(a2d.shape[1]), row(c2d.shape[1]), const(wout_b), const(gmlp), const(w1_b),
                  const(w2_b), per_seq(rows2, seqs_per_step), per_seq(t_new, seqs_per_step),
                  per_seq(t_new, seqs_per_step)]
        + [const(x) for x in lams] + [const(subln), any_spec, any_spec],
        out_specs=[row(d), per_seq(t_new, seqs_per_step)],
        scratch_shapes=scratch,
    )
    return pl.pallas_call(
        kern,
        grid_spec=grid_spec,
        out_shape=[jax.ShapeDtypeStruct((t, d), F32), jax.ShapeDtypeStruct((n_seqs, t_new, a), F32)],
        compiler_params=pltpu.CompilerParams(dimension_semantics=("arbitrary",),
                                             vmem_limit_bytes=VMEM_LIMIT_BYTES),
        name="finish_sample",
    )(page_table, x2d, a2d, c2d, wout_b, gmlp, w1_b, w2_b, q2, kn, vn, *lams, subln, cache_kt, cache_v)


def kernel(x_prompt, x_sample, cache_k, cache_v, state_conv, page_table, norm_mix, w_in, q_norm, k_norm,
           lambda_q1, lambda_k1, lambda_q2, lambda_k2, subln, conv_w, w_out, norm_mlp, w_ff1, w_ff2):
    depth, n_pool, page, n_heads, _, dqk = cache_k.shape
    dv = cache_v.shape[-1]
    attn_w = n_heads * dv
    conv_width = state_conv.shape[-1]
    conv_k = conv_w.shape[1]
    batch, seq, d_model = x_prompt.shape
    dec_batch, dec_seq, _ = x_sample.shape
    q_scale = dqk ** -0.5
    n_groups = attn_w // dqk

    gmat = (jnp.kron(jnp.eye(n_groups, dtype=F32), jnp.ones((dqk, dqk), F32)) / dqk).astype(BF16)

    xp = x_prompt.reshape(batch * seq, d_model)
    xs = x_sample.reshape(dec_batch * dec_seq, d_model)
    outs = {name: [] for name in ("kp", "vp", "cp", "ks", "vs", "cs")}
    for l in range(depth):
        lam_init = _lambda_init(l)
        win_b = w_in[l].astype(BF16)
        wout_b = w_out[l].astype(BF16)
        w1_b = w_ff1[l].astype(BF16)
        w2_b = w_ff2[l].astype(BF16)
        gmix = norm_mix[l][None]
        gmlp = norm_mlp[l][None]
        gq = jnp.tile(q_norm[l], n_groups)[None]
        gk = jnp.tile(k_norm[l], n_groups)[None]
        lams = [lambda_q1[l][None], lambda_k1[l][None], lambda_q2[l][None], lambda_k2[l][None]]
        sub = subln[l][None]
        cw = conv_w[l]
        pre = dict(attn_w=attn_w, conv_w=conv_width, n_heads=n_heads, q_scale=q_scale)
        fin = functools.partial(_finish, ff_chunk=1024)

        st = state_conv[l]
        hist = jnp.stack([
            jnp.concatenate([st[:, conv_k - 1 - s:, :],
                             jnp.zeros((dec_batch, dec_seq - s, conv_width), F32)], axis=1)
            .reshape(dec_batch * dec_seq, conv_width)
            for s in range(1, conv_k)])
        ts = dec_batch * dec_seq
        qb_s, k, v, kb_s, c_s, u = _prestage_sample(xs, gmix, win_b, gq, gk, gmat, cw, hist, seq_len=dec_seq,
                                                    **pre)
        q2 = jnp.tile(qb_s.reshape(dec_batch, dec_seq, attn_w), (1, 2, 1))
        kn = kb_s.reshape(dec_batch, dec_seq, attn_w)
        vn = v.reshape(dec_batch, dec_seq, attn_w)
        cache_kt = jnp.transpose(cache_k[l], (0, 2, 3, 4, 1)).reshape(n_pool, attn_w, page)
        cache_vr = cache_v[l].reshape(n_pool, page * n_heads, dv)

        tm_p, tk_p, tm_f, tiles_a = 512, 256, 512, 1
        n_pre = batch * seq // tm_p
        n_att = batch * seq // (tk_p * tiles_a)
        assert dec_batch > n_pre + n_att
        share = [slice(0, n_pre), slice(n_pre, n_pre + n_att), slice(n_pre + n_att, dec_batch)]
        stream_in = lambda sl: (page_table[sl], q2[sl], kn[sl], vn[sl])
        stream_kw = dict(n_heads=n_heads, dqk=dqk, ch=16, lam_init=lam_init)

        wqkt_b = w_in[l][:, :2 * attn_w].T.astype(BF16)
        gqt = jnp.broadcast_to(gq.reshape(attn_w, 1), (attn_w, tm_p))
        gkt = jnp.broadcast_to(gk.reshape(attn_w, 1), (attn_w, tm_p))
        qtb, kt, kb, v_il, vtb, c, c_state, a_s0 = _prestage_prompt(
            xp, gmix, win_b, wqkt_b, gqt, gkt, gmat, cw,
            *stream_in(share[0]), lams, sub, cache_kt, cache_vr,
            attn_w=attn_w, conv_w=conv_width, q_scale=q_scale, tm=tm_p, tk=tk_p, seq_len=seq, n_slots=2,
            **stream_kw)
        outs["kp"].append(jnp.transpose(kt.reshape(batch, n_heads, 2, dqk, seq), (0, 4, 1, 2, 3)))
        outs["vp"].append(v_il.reshape(batch, seq, n_heads, dv))
        outs["cp"].append(c_state)
        a_p, a_s1 = _prompt_attention(qtb, kb.reshape(batch, seq, attn_w), vtb, lams, sub,
                                      *stream_in(share[1]), cache_kt, cache_vr,
                                      tiles=tiles_a, n_slots=4, **stream_kw)
        xp, a_s2 = _finish_sample(xp, a_p.reshape(batch * seq, attn_w), c, wout_b, gmlp, w1_b, w2_b,
                                  *stream_in(share[2]), lams, sub, cache_kt, cache_vr,
                                  tm=tm_f, ff_chunk=1024, n_slots=2, **stream_kw)
        a_s = jnp.concatenate([a_s0, a_s1, a_s2], axis=0)
        xs = fin(xs, a_s.reshape(ts, attn_w), c_s, wout_b, gmlp, w1_b, w2_b, tm=256)
        outs["ks"].append(k.reshape(dec_batch, dec_seq, n_heads, 2, dqk))
        outs["vs"].append(v.reshape(dec_batch, dec_seq, n_heads, dv))
        outs["cs"].append(u.reshape(dec_batch, dec_seq, conv_width)[:, dec_seq - (conv_k - 1):, :])

    return (xp.reshape(batch, seq, d_model), xs.reshape(dec_batch, dec_seq, d_model),
            jnp.stack(outs["kp"]), jnp.stack(outs["vp"]), jnp.stack(outs["cp"]),
            jnp.stack(outs["ks"]), jnp.stack(outs["vs"]), jnp.stack(outs["cs"]))
```

```python
import functools
import math

import jax
import jax.numpy as jnp
from jax import lax
from jax.experimental import pallas as pl
from jax.experimental.pallas import tpu as pltpu

F32 = jnp.float32
BF16 = jnp.bfloat16
EPS = 1e-6
NEG_INF = -1e30

V7X_VMEM_BYTES = 64 * 1024 * 1024
V7X_SUBLANES = 8
VMEM_LIMIT_BYTES = V7X_VMEM_BYTES - 8 * 1024 * 1024


def _lambda_init(layer):
    return 0.8 - 0.6 * math.exp(-0.3 * layer)


def _rms_rows(x, g):
    ms = jnp.mean(x * x, axis=-1, keepdims=True)
    return x * lax.rsqrt(ms + EPS) * g


def _const_spec(shape):
    zeros = (0,) * len(shape)
    return pl.BlockSpec(shape, lambda *_: zeros, pipeline_mode=pl.Buffered(1))


def _lam_value(lq1_ref, lk1_ref, lq2_ref, lk2_ref, lam_init):
    s1 = jnp.sum(lq1_ref[...] * lk1_ref[...], axis=-1, keepdims=True)
    s2 = jnp.sum(lq2_ref[...] * lk2_ref[...], axis=-1, keepdims=True)
    return jnp.exp(s1) - jnp.exp(s2) + lam_init


def _prestage_parts(refs, *, attn_w, conv_w, n_heads, seq_tiles, seq_len, q_scale, sample):
    if sample:
        (x_ref, gmix_ref, win_ref, gq_ref, gk_ref, gmat_ref, cw_ref, hist_ref,
         qb_ref, k_ref, v_ref, kb_ref, c_ref, u_ref, ubuf) = refs
    else:
        (x_ref, gmix_ref, win_ref, wqkt_ref, gqt_ref, gkt_ref, gmat_ref, cw_ref,
         qtb_ref, kt_ref, kb_ref, vil_ref, vtb_ref, c_ref, cs_ref, ubuf) = refs
        wqt_ref, wkt_ref = wqkt_ref.at[0:attn_w], wqkt_ref.at[attn_w:2 * attn_w]
    tm = x_ref.shape[0]
    conv_k = cw_ref.shape[0]
    a, c = attn_w, conv_w
    dv = a // n_heads
    st = {}

    def proj(lo, width):
        return jnp.dot(st["h"], win_ref[:, lo:lo + width], preferred_element_type=F32)

    def group_rms(z, g):
        msq = jnp.dot((z * z).astype(BF16), gmat_ref[...], preferred_element_type=F32)
        return z * lax.rsqrt(msq + EPS) * g

    def group_rms_t(wt_ref, gt_ref):
        zt = lax.dot_general(wt_ref[...], st["h"], (((1,), (1,)), ((), ())), preferred_element_type=F32)
        msq = jnp.dot(gmat_ref[...], (zt * zt).astype(BF16), preferred_element_type=F32)
        return zt * lax.rsqrt(msq + EPS) * gt_ref[...]

    def queries():
        st["h"] = _rms_rows(x_ref[...], gmix_ref[...]).astype(BF16)
        if sample:
            qb_ref[...] = (group_rms(proj(0, a), gq_ref[...]) * q_scale).astype(BF16)
        else:
            qtb_ref[0] = (group_rms_t(wqt_ref, gqt_ref) * q_scale).astype(BF16)

    def keys():
        if sample:
            k = group_rms(proj(a, a), gk_ref[...])
            k_ref[...] = k
            kb_ref[...] = k.astype(BF16)
        else:
            kt = group_rms_t(wkt_ref, gkt_ref)
            kt_ref[0] = kt
            kb_ref[...] = kt.T.astype(BF16)

    def values():
        v = proj(2 * a, a)
        if sample:
            v_ref[...] = v
        else:
            for hh in range(n_heads):
                vil_ref[pl.ds(hh, tm, stride=n_heads), :] = v[:, hh * dv:(hh + 1) * dv]
            vt = v.T
            tk = vtb_ref.shape[-1]
            for t in range(tm // tk):
                vtb_ref[0, t] = vt[:, t * tk:(t + 1) * tk].astype(BF16)

    def conv():
        b_gate = proj(3 * a, c)
        u = proj(3 * a + c, c) * proj(3 * a + 2 * c, c)
        halo = V7X_SUBLANES
        if sample:
            ubuf[0:halo, :] = jnp.zeros((halo, c), F32)
        else:
            first = pl.program_id(0) % seq_tiles == 0

            @pl.when(first)
            def _():
                ubuf[0:halo, :] = jnp.zeros((halo, c), F32)

            @pl.when(jnp.logical_not(first))
            def _():
                ubuf[0:halo, :] = ubuf[tm:tm + halo, :]

        ubuf[halo:halo + tm, :] = u
        if sample:
            t_pos = lax.broadcasted_iota(jnp.int32, (tm, 1), 0) % seq_len
        y = None
        for j in range(conv_k):
            shift = conv_k - 1 - j
            if shift == 0:
                tap = u
            else:
                tap = ubuf[halo - shift:halo - shift + tm, :]
                if sample:
                    tap = jnp.where(t_pos >= shift, tap, hist_ref[shift - 1])
            term = cw_ref[j:j + 1, :] * tap
            y = term if y is None else y + term
        c_ref[...] = (b_gate * y).astype(c_ref.dtype)
        if sample:
            u_ref[...] = u
        else:
            cs_ref[0] = u[tm - (conv_k - 1):tm, :]

    return [queries, keys, values, conv]


def _prestage_kernel(*refs, **kw):
    for part in _prestage_parts(refs, **kw):
        part()


def _prestage_stream_kernel(pt_ref, *refs, n_in, n_out, seq0, dqk, page, ch, n_chunks, lam_init, **kw):
    pre_in, rest = refs[:n_in], refs[n_in:]
    (lq1_ref, lk1_ref, lq2_ref, lk2_ref, sub_ref, q_ref, kn_ref, vn_ref, ckt_hbm, cv_hbm), rest = rest[:10], rest[10:]
    pre_out, (os_ref, ubuf, kbuf, vbuf, ksem, vsem) = rest[:n_out], rest[n_out:]
    step = pl.program_id(0)
    seqs_per_step = q_ref.shape[0]
    n_heads = kw["n_heads"]
    stream = _SampleStream(pt_ref, ckt_hbm, cv_hbm, kbuf, vbuf, ksem, vsem, seq0=seq0,
                           n_seqs=pl.num_programs(0) * seqs_per_step, n_heads=n_heads, dqk=dqk,
                           dv=q_ref.shape[-1] // n_heads, page=page, ch=ch, n_chunks=n_chunks,
                           t_new=kn_ref.shape[1])

    @pl.when(step == 0)
    def _():
        stream.prologue()

    parts = _prestage_parts(tuple(pre_in) + tuple(pre_out) + (ubuf,), **kw)
    lam = _lam_value(lq1_ref, lk1_ref, lq2_ref, lk2_ref, lam_init)
    _stream_sequences(stream, step * seqs_per_step, q_ref, kn_ref, vn_ref, os_ref, lam, sub_ref[...], lam_init,
                      parts)


def _prestage_sample(x2d, gmix, win_b, gq, gk, gmat, cw, hist, *, attn_w, conv_w, n_heads, seq_len, q_scale):
    t, d = x2d.shape
    assert t % seq_len == 0
    args = [x2d, gmix, win_b, gq, gk, gmat, cw, hist]
    out_shape = [jax.ShapeDtypeStruct((t, attn_w), BF16), jax.ShapeDtypeStruct((t, attn_w), F32),
                 jax.ShapeDtypeStruct((t, attn_w), F32), jax.ShapeDtypeStruct((t, attn_w), BF16),
                 jax.ShapeDtypeStruct((t, conv_w), F32), jax.ShapeDtypeStruct((t, conv_w), F32)]
    kern = functools.partial(_prestage_kernel, attn_w=attn_w, conv_w=conv_w, n_heads=n_heads, seq_tiles=1,
                             seq_len=seq_len, q_scale=q_scale, sample=True)
    return pl.pallas_call(
        kern,
        grid=(1,),
        in_specs=[_const_spec(x.shape) for x in args],
        out_specs=[pl.BlockSpec(s.shape, lambda i: (0, 0)) for s in out_shape],
        out_shape=out_shape,
        scratch_shapes=[pltpu.VMEM((t + 2 * V7X_SUBLANES, conv_w), F32)],
        compiler_params=pltpu.CompilerParams(dimension_semantics=("arbitrary",),
                                             vmem_limit_bytes=VMEM_LIMIT_BYTES),
        name="prestage_sample",
    )(*args)


def _prestage_prompt(x2d, gmix, win_b, wqkt_b, gqt, gkt, gmat, cw,
                     page_table, q2, kn, vn, lams, subln, cache_kt, cache_v,
                     *, attn_w, conv_w, n_heads, dqk, tm, tk, seq_len, q_scale, ch, n_slots, lam_init, seq0,
                     n_host):
    t, d = x2d.shape
    conv_k = cw.shape[0]
    dv = attn_w // n_heads
    assert seq_len % tm == 0 and tm % tk == 0 and gkt.shape == (attn_w, tm) and gqt.shape == (attn_w, tm)
    seq_tiles = seq_len // tm
    n_seq = t // seq_len
    n_steps = t // tm
    n_seqs, rows2 = n_host, q2.shape[1]
    t_new = rows2 // 2
    page = cache_kt.shape[-1]
    n_pages = page_table.shape[1]
    assert n_pages % ch == 0 and n_seqs % n_steps == 0
    seqs_per_step = n_seqs // n_steps
    row = lambda width: pl.BlockSpec((tm, width), lambda i, pt: (i, 0))
    col = pl.BlockSpec((1, attn_w, tm), lambda i, pt: (i // seq_tiles, 0, i % seq_tiles))
    const = lambda x: pl.BlockSpec(x.shape, lambda i, pt: (0,) * x.ndim, pipeline_mode=pl.Buffered(1))
    per_seq_in, per_seq, any_spec, scratch = _stream_specs(q2, page, ch, n_slots, n_heads, dv, lambda i: i, seq0)
    pre_args = [x2d, gmix, win_b, wqkt_b, gqt, gkt, gmat, cw]
    stream_args = [*lams, subln, q2, kn, vn, cache_kt, cache_v]
    in_specs = ([row(d)] + [const(x) for x in pre_args[1:]] + [const(x) for x in lams] + [const(subln)]
                + [per_seq_in(rows2, seqs_per_step), per_seq_in(t_new, seqs_per_step),
                   per_seq_in(t_new, seqs_per_step), any_spec, any_spec])
    out_shape = [jax.ShapeDtypeStruct((n_seq, attn_w, seq_len), BF16),
                 jax.ShapeDtypeStruct((n_seq, attn_w, seq_len), F32),
                 jax.ShapeDtypeStruct((t, attn_w), BF16),
                 jax.ShapeDtypeStruct((t * n_heads, dv), F32),
                 jax.ShapeDtypeStruct((n_seq, seq_len // tk, attn_w, tk), BF16),
                 jax.ShapeDtypeStruct((t, conv_w), BF16),
                 jax.ShapeDtypeStruct((n_seq, conv_k - 1, conv_w), F32),
                 jax.ShapeDtypeStruct((n_seqs, t_new, attn_w), F32)]
    out_specs = [col, col, row(attn_w),
                 pl.BlockSpec((tm * n_heads, dv), lambda i, pt: (i, 0)),
                 pl.BlockSpec((1, tm // tk, attn_w, tk), lambda i, pt: (i // seq_tiles, i % seq_tiles, 0, 0)),
                 row(conv_w),
                 pl.BlockSpec((1, conv_k - 1, conv_w), lambda i, pt: (i // seq_tiles, 0, 0)),
                 per_seq(t_new, seqs_per_step)]
    kern = functools.partial(_prestage_stream_kernel, n_in=len(pre_args), n_out=len(out_shape) - 1, seq0=seq0, dqk=dqk,
                             page=page, ch=ch, n_chunks=n_pages // ch, lam_init=lam_init,
                             attn_w=attn_w, conv_w=conv_w, n_heads=n_heads, seq_tiles=seq_tiles,
                             seq_len=seq_len, q_scale=q_scale, sample=False)
    grid_spec = pltpu.PrefetchScalarGridSpec(
        num_scalar_prefetch=1,
        grid=(n_steps,),
        in_specs=in_specs,
        out_specs=out_specs,
        scratch_shapes=[pltpu.VMEM((tm + 2 * V7X_SUBLANES, conv_w), F32)] + scratch,
    )
    return pl.pallas_call(
        kern,
        grid_spec=grid_spec,
        out_shape=out_shape,
        compiler_params=pltpu.CompilerParams(dimension_semantics=("arbitrary",),
                                             vmem_limit_bytes=VMEM_LIMIT_BYTES),
        name="prestage_prompt_sample",
    )(page_table, *pre_args, *stream_args)


def _prompt_attn_parts(i, off, qt_ref, k_ref, vt_ref, sub_ref, o_ref, lam, *, n_heads, dqk, lam_init):
    tq = tk = vt_ref.shape[-1]
    dv = qt_ref.shape[1] // n_heads
    st = {}

    def prepare():
        row = lax.broadcasted_iota(jnp.int32, (dv, tq), 0)
        qst = []
        for h in range(n_heads):
            qh = qt_ref[0, h * dv:(h + 1) * dv, off:off + tq]
            zero = jnp.zeros_like(qh)
            qst.append(jnp.concatenate([jnp.where(row < dqk, qh, zero), jnp.where(row >= dqk, qh, zero)],
                                       axis=1))
        st["qst"] = qst

    def update(j, carry, masked):
        qst = st["qst"]
        start = pl.multiple_of(j * tk, tk)

        def score(h):
            kj = k_ref[0, pl.ds(start, tk), h * dv:(h + 1) * dv]
            s = jnp.dot(kj, qst[h], preferred_element_type=F32)
            if masked:
                key = lax.broadcasted_iota(jnp.int32, s.shape, 0)
                qry = lax.broadcasted_iota(jnp.int32, s.shape, 1) % tq
                s = jnp.where(qry >= key, s, NEG_INF)
            return s

        def softmax(h, s):
            m, l, _ = carry[h]
            m_new = jnp.maximum(m, jnp.max(s, axis=0, keepdims=True))
            p = jnp.exp(s - m_new)
            alpha = jnp.exp(m - m_new)
            l = alpha * l + jnp.sum(p, axis=0, keepdims=True)
            return m_new, l, alpha, p.astype(BF16)

        def weigh(h, m_new, l, alpha, p):
            vtj = vt_ref[0, j, h * dv:(h + 1) * dv, :]
            acc = alpha * carry[h][2] + jnp.dot(vtj, p, preferred_element_type=F32)
            return m_new, l, acc

        scores = [score(h) for h in range(n_heads)]
        probs = [softmax(h, s) for h, s in enumerate(scores)]
        return tuple(weigh(h, *pr) for h, pr in enumerate(probs))

    def below_diagonal():
        init = tuple((jnp.full((1, 2 * tq), NEG_INF, F32), jnp.zeros((1, 2 * tq), F32),
                      jnp.zeros((dv, 2 * tq), F32)) for _ in range(n_heads))
        st["carry"] = lax.fori_loop(0, i, lambda j, c: update(j, c, False), init)

    def diagonal():
        st["carry"] = update(i, st["carry"], True)

    def finish():
        for h in range(n_heads):
            m, l, acc = st["carry"][h]
            ot = acc[:, :tq] / l[:, :tq] - lam * (acc[:, tq:] / l[:, tq:])
            o = _rms_rows(ot.T, sub_ref[...]) * (1.0 - lam_init)
            o_ref[0, off:off + tq, h * dv:(h + 1) * dv] = o.astype(o_ref.dtype)

    return [prepare, below_diagonal, diagonal, finish]


def _stream_sequences(stream, first_seq, q_ref, kn_ref, vn_ref, os_ref, lam, sub, lam_init, parts):
    dv = stream.dv
    for k in range(q_ref.shape[0]):
        qbd, state = stream.init_state(q_ref[k].astype(F32), kn_ref[k].astype(F32), vn_ref[k])
        for chunk in range(stream.n_chunks):
            stream.wait(chunk % stream.n_slots)
            state = stream.consume(chunk % stream.n_slots, qbd, state)
            stream.start_after(first_seq + k, chunk)
            if parts:
                parts.pop(0)()

        def store(h, o, k=k):
            os_ref[k, :, h * dv:(h + 1) * dv] = o

        stream.finalize(state, lam, sub, lam_init, store)
    while parts:
        parts.pop(0)()


def _prompt_attn_kernel(pt_ref, qt_ref, k_ref, vt_ref, lq1_ref, lk1_ref, lq2_ref, lk2_ref, sub_ref,
                        q_ref, kn_ref, vn_ref, ckt_hbm, cv_hbm, o_ref, os_ref, kbuf, vbuf, ksem, vsem,
                        *, seq0, n_heads, dqk, page, ch, n_chunks, lam_init):
    step = pl.program_id(0) * pl.num_programs(1) + pl.program_id(1)
    n_steps = pl.num_programs(0) * pl.num_programs(1)
    seqs_per_step = q_ref.shape[0]
    stream = _SampleStream(pt_ref, ckt_hbm, cv_hbm, kbuf, vbuf, ksem, vsem, seq0=seq0,
                           n_seqs=n_steps * seqs_per_step,
                           n_heads=n_heads, dqk=dqk, dv=q_ref.shape[-1] // n_heads, page=page, ch=ch,
                           n_chunks=n_chunks, t_new=kn_ref.shape[1])

    @pl.when(step == 0)
    def _():
        stream.prologue()

    lam = _lam_value(lq1_ref, lk1_ref, lq2_ref, lk2_ref, lam_init)
    tq = vt_ref.shape[-1]
    tiles = qt_ref.shape[-1] // tq
    parts = []
    for t in range(tiles):
        parts += _prompt_attn_parts(pl.program_id(1) * tiles + t, t * tq, qt_ref, k_ref, vt_ref, sub_ref, o_ref,
                                    lam, n_heads=n_heads, dqk=dqk, lam_init=lam_init)
    _stream_sequences(stream, step * seqs_per_step, q_ref, kn_ref, vn_ref, os_ref, lam, sub_ref[...], lam_init,
                      parts)


def _stream_specs(q2, page, ch, n_slots, n_heads, dv, index, seq0):
    a = q2.shape[-1]

    def per_seq(r, n, offset=0):
        assert offset % n == 0
        return pl.BlockSpec((n, r, a), lambda *ids: (index(*ids[:-1]) + offset // n, 0, 0))

    per_seq_in = lambda r, n: per_seq(r, n, seq0)
    any_spec = pl.BlockSpec(memory_space=pl.ANY)
    scratch = [pltpu.VMEM((n_slots, ch, a, page), F32), pltpu.VMEM((n_slots, ch * page * n_heads, dv), F32),
               pltpu.SemaphoreType.DMA((n_slots,)), pltpu.SemaphoreType.DMA((n_slots,))]
    return per_seq_in, per_seq, any_spec, scratch


def _prompt_attention(qtb, kb, vtb, lams, subln, page_table, q2, kn, vn, cache_kt, cache_v,
                      *, tiles, n_heads, dqk, ch, n_slots, lam_init, seq0, n_host):
    b, a, s = qtb.shape
    nk, tk = vtb.shape[1], vtb.shape[3]
    nq = s // (tk * tiles)
    n_seqs, rows2 = n_host, q2.shape[1]
    t_new = rows2 // 2
    page = cache_kt.shape[-1]
    dv = cache_v.shape[-1]
    n_pages = page_table.shape[1]
    assert n_pages % ch == 0 and n_seqs % (b * nq) == 0 and s % (tk * tiles) == 0
    seqs_per_step = n_seqs // (b * nq)
    kern = functools.partial(_prompt_attn_kernel, seq0=seq0, n_heads=n_heads, dqk=dqk, page=page, ch=ch,
                             n_chunks=n_pages // ch, lam_init=lam_init)
    per_seq_in, per_seq, any_spec, scratch = _stream_specs(q2, page, ch, n_slots, n_heads, dv,
                                                           lambda bi, i: bi * nq + i, seq0)
    const = lambda x: pl.BlockSpec(x.shape, lambda bi, i, pt: (0,) * x.ndim, pipeline_mode=pl.Buffered(1))
    grid_spec = pltpu.PrefetchScalarGridSpec(
        num_scalar_prefetch=1,
        grid=(b, nq),
        in_specs=[pl.BlockSpec((1, a, tk * tiles), lambda bi, i, pt: (bi, 0, i)),
                  pl.BlockSpec((1, s, a), lambda bi, i, pt: (bi, 0, 0)),
                  pl.BlockSpec((1, nk, a, tk), lambda bi, i, pt: (bi, 0, 0, 0))]
        + [const(x) for x in lams] + [const(subln)]
        + [per_seq_in(rows2, seqs_per_step), per_seq_in(t_new, seqs_per_step), per_seq_in(t_new, seqs_per_step),
           any_spec, any_spec],
        out_specs=[pl.BlockSpec((1, tk * tiles, a), lambda bi, i, pt: (bi, i, 0)),
                   per_seq(t_new, seqs_per_step)],
        scratch_shapes=scratch,
    )
    return pl.pallas_call(
        kern,
        grid_spec=grid_spec,
        out_shape=[jax.ShapeDtypeStruct((b, s, a), BF16), jax.ShapeDtypeStruct((n_seqs, t_new, a), F32)],
        compiler_params=pltpu.CompilerParams(dimension_semantics=("arbitrary",) * 2,
                                             vmem_limit_bytes=VMEM_LIMIT_BYTES),
        name="prompt_attn_sample",
    )(page_table, qtb, kb, vtb, *lams, subln, q2, kn, vn, cache_kt, cache_v)


def _k_page_copy(ckt_hbm, kbuf, sem, page_idx, slot, p):
    return pltpu.make_async_copy(ckt_hbm.at[page_idx], kbuf.at[slot, p], sem.at[slot])


def _v_page_copy(cv_hbm, vbuf, sem, page_idx, slot, p, rows):
    return pltpu.make_async_copy(cv_hbm.at[page_idx], vbuf.at[slot, pl.ds(p * rows, rows), :], sem.at[slot])


class _SampleStream:
    def __init__(self, pt_ref, ckt_hbm, cv_hbm, kbuf, vbuf, ksem, vsem, *, seq0, n_seqs, n_heads, dqk, dv, page,
                 ch, n_chunks, t_new):
        self.n_slots = kbuf.shape[0]
        self.seq0 = seq0
        assert n_chunks % self.n_slots == 0 and n_heads % 2 == 0
        self.pt_ref, self.ckt_hbm, self.cv_hbm = pt_ref, ckt_hbm, cv_hbm
        self.kbuf, self.vbuf, self.ksem, self.vsem = kbuf, vbuf, ksem, vsem
        self.n_seqs, self.n_heads, self.dqk, self.dv = n_seqs, n_heads, dqk, dv
        self.page, self.ch, self.n_chunks, self.t_new = page, ch, n_chunks, t_new

    def start(self, seq, chunk, slot):
        for p in range(self.ch):
            pg = self.pt_ref[self.seq0 + seq, chunk * self.ch + p]
            _k_page_copy(self.ckt_hbm, self.kbuf, self.ksem, pg, slot, p).start()
            _v_page_copy(self.cv_hbm, self.vbuf, self.vsem, pg, slot, p, self.page * self.n_heads).start()

    def wait(self, slot):
        for p in range(self.ch):
            _k_page_copy(self.ckt_hbm, self.kbuf, self.ksem, 0, slot, p).wait()
            _v_page_copy(self.cv_hbm, self.vbuf, self.vsem, 0, slot, p, self.page * self.n_heads).wait()

    def prologue(self):
        for chunk in range(self.n_slots):
            self.start(0, chunk, chunk)

    def start_after(self, seq, chunk):
        slot = chunk % self.n_slots
        nxt = chunk + self.n_slots
        if nxt < self.n_chunks:
            self.start(seq, nxt, slot)
        else:
            @pl.when(seq + 1 < self.n_seqs)
            def _():
                self.start(seq + 1, nxt - self.n_chunks, slot)

    def init_state(self, q2, kn, vn):
        t_new, dv, dqk = self.t_new, self.dv, self.dqk
        rows2 = 2 * t_new
        lane = lax.broadcasted_iota(jnp.int32, (rows2, dv), 1)
        rowi = lax.broadcasted_iota(jnp.int32, (rows2, dv), 0)
        lane_lo = jnp.where(rowi < t_new, 0, dqk)
        comp_mask = jnp.logical_and(lane >= lane_lo, lane < lane_lo + dqk)
        t_row = lax.broadcasted_iota(jnp.int32, (rows2, 1), 0) % t_new
        zero = jnp.zeros((rows2, dv), F32)
        qbd, state = [], []
        for pair in range(self.n_heads // 2):
            per_head = []
            for h in (2 * pair, 2 * pair + 1):
                qf = jnp.where(comp_mask, q2[:, h * dv:(h + 1) * dv], 0.0)
                s_new = [jnp.sum(qf * kn[j:j + 1, h * dv:(h + 1) * dv], axis=-1, keepdims=True)
                         for j in range(t_new)]
                valid = [t_row >= j for j in range(t_new)]
                m = s_new[0]
                for j in range(1, t_new):
                    m = jnp.maximum(m, jnp.where(valid[j], s_new[j], NEG_INF))
                l = jnp.zeros((rows2, 1), F32)
                acc = zero
                for j in range(t_new):
                    pj = jnp.where(valid[j], jnp.exp(s_new[j] - m), 0.0)
                    l = l + pj
                    acc = acc + pj * vn[j:j + 1, h * dv:(h + 1) * dv]
                per_head.append((qf, m, l, acc))
            (q0, m0, l0, a0), (q1, m1, l1, a1) = per_head
            qbd.append(jnp.concatenate([jnp.concatenate([q0, zero], axis=1),
                                        jnp.concatenate([zero, q1], axis=1)], axis=0))
            state.append((jnp.concatenate([m0, m1], axis=0), jnp.concatenate([l0, l1], axis=0),
                          jnp.concatenate([jnp.concatenate([a0, zero], axis=1),
                                           jnp.concatenate([zero, a1], axis=1)], axis=0)))
        return qbd, state

    def consume(self, slot, qbd, state):
        dv = self.dv
        tokens = self.ch * self.page
        scores = []
        for pair in range(len(state)):
            kt = jnp.concatenate([self.kbuf[slot, p, 2 * pair * dv:(2 * pair + 2) * dv, :]
                                  for p in range(self.ch)], axis=1)
            scores.append(jnp.dot(qbd[pair], kt, preferred_element_type=F32))
        probs = []
        for s, (m, l, _) in zip(scores, state):
            m_new = jnp.maximum(m, jnp.max(s, axis=-1, keepdims=True))
            p = jnp.exp(s - m_new)
            alpha = jnp.exp(m - m_new)
            probs.append((m_new, alpha * l + jnp.sum(p, axis=-1, keepdims=True), alpha, p))
        out = []
        for pair, (m_new, l, alpha, p) in enumerate(probs):
            h0, h1 = 2 * pair, 2 * pair + 1
            v2 = jnp.concatenate([self.vbuf[slot, pl.ds(h0, tokens, stride=self.n_heads), :],
                                  self.vbuf[slot, pl.ds(h1, tokens, stride=self.n_heads), :]], axis=1)
            acc = alpha * state[pair][2] + jnp.dot(p, v2, preferred_element_type=F32)
            out.append((m_new, l, acc))
        return out

    def finalize(self, state, lam, sub, lam_init, store):
        t_new, dv = self.t_new, self.dv
        rows2 = 2 * t_new
        for pair, (m, l, acc) in enumerate(state):
            for k in range(2):
                a_h = acc[k * rows2:(k + 1) * rows2, k * dv:(k + 1) * dv]
                l_h = l[k * rows2:(k + 1) * rows2]
                o = a_h[:t_new] / l_h[:t_new] - lam * (a_h[t_new:] / l_h[t_new:])
                store(2 * pair + k, _rms_rows(o, sub) * (1.0 - lam_init))


def _finish_parts(x_ref, a_ref, c_ref, wout_ref, gmlp_ref, w1_ref, w2_ref, y_ref, ff_chunk):
    a_w = a_ref.shape[1]
    d_ff = w1_ref.shape[1]
    st = {}

    def project():
        mix = jnp.dot(a_ref[...].astype(BF16), wout_ref[0:a_w, :], preferred_element_type=F32)
        mix = mix + jnp.dot(c_ref[...].astype(BF16), wout_ref[a_w:, :], preferred_element_type=F32)
        st["acc"] = x_ref[...] + mix
        st["h"] = _rms_rows(st["acc"], gmlp_ref[...]).astype(BF16)

    def ff(lo):
        z = jnp.dot(st["h"], w1_ref[:, lo:lo + ff_chunk], preferred_element_type=F32)
        z = jnp.maximum(z, 0.0)
        z = (z * z).astype(BF16)
        st["acc"] = st["acc"] + jnp.dot(z, w2_ref[lo:lo + ff_chunk, :], preferred_element_type=F32)

    def store():
        y_ref[...] = st["acc"]

    return [project] + [functools.partial(ff, lo) for lo in range(0, d_ff, ff_chunk)] + [store]


def _finish_kernel(x_ref, a_ref, c_ref, wout_ref, gmlp_ref, w1_ref, w2_ref, y_ref, *, ff_chunk):
    for part in _finish_parts(x_ref, a_ref, c_ref, wout_ref, gmlp_ref, w1_ref, w2_ref, y_ref, ff_chunk):
        part()


def _finish_sample_kernel(pt_ref, x_ref, a_ref, c_ref, wout_ref, gmlp_ref, w1_ref, w2_ref,
                          q_ref, kn_ref, vn_ref, lq1_ref, lk1_ref, lq2_ref, lk2_ref, sub_ref, ckt_hbm, cv_hbm,
                          y_ref, os_ref, kbuf, vbuf, ksem, vsem,
                          *, seq0, ff_chunk, n_heads, dqk, page, ch, n_chunks, lam_init):
    step = pl.program_id(0)
    seqs_per_step = q_ref.shape[0]
    t_new = kn_ref.shape[1]
    dv = q_ref.shape[-1] // n_heads
    stream = _SampleStream(pt_ref, ckt_hbm, cv_hbm, kbuf, vbuf, ksem, vsem, seq0=seq0,
                           n_seqs=pl.num_programs(0) * seqs_per_step, n_heads=n_heads, dqk=dqk, dv=dv,
                           page=page, ch=ch, n_chunks=n_chunks, t_new=t_new)

    @pl.when(step == 0)
    def _():
        stream.prologue()

    parts = _finish_parts(x_ref, a_ref, c_ref, wout_ref, gmlp_ref, w1_ref, w2_ref, y_ref, ff_chunk)
    lam = _lam_value(lq1_ref, lk1_ref, lq2_ref, lk2_ref, lam_init)
    _stream_sequences(stream, step * seqs_per_step, q_ref, kn_ref, vn_ref, os_ref, lam, sub_ref[...], lam_init,
                      parts)


def _finish(x2d, a2d, c2d, wout_b, gmlp, w1_b, w2_b, *, tm, ff_chunk):
    t, d = x2d.shape
    row = lambda width: pl.BlockSpec((tm, width), lambda i: (i, 0))
    return pl.pallas_call(
        functools.partial(_finish_kernel, ff_chunk=ff_chunk),
        grid=(t // tm,),
        in_specs=[row(d), row(a2d.shape[1]), row(c2d.shape[1]), _const_spec(wout_b.shape),
                  _const_spec(gmlp.shape), _const_spec(w1_b.shape), _const_spec(w2_b.shape)],
        out_specs=row(d),
        out_shape=jax.ShapeDtypeStruct((t, d), F32),
        compiler_params=pltpu.CompilerParams(dimension_semantics=("arbitrary",),
                                             vmem_limit_bytes=VMEM_LIMIT_BYTES),
        name="finish",
    )(x2d, a2d, c2d, wout_b, gmlp, w1_b, w2_b)


def _finish_sample(x2d, a2d, c2d, wout_b, gmlp, w1_b, w2_b, page_table, q2, kn, vn, lams, subln, cache_kt, cache_v,
                   *, tm, ff_chunk, n_heads, dqk, ch, n_slots, lam_init, seq0, n_host):
    t, d = x2d.shape
    n_seqs, rows2, a = n_host, q2.shape[1], q2.shape[2]
    t_new = rows2 // 2
    n_pool, _, page = cache_kt.shape
    dv = cache_v.shape[-1]
    n_pages = page_table.shape[1]
    n_steps = t // tm
    assert n_pages % ch == 0 and n_seqs % n_steps == 0
    n_chunks = n_pages // ch
    seqs_per_step = n_seqs // n_steps
    kern = functools.partial(_finish_sample_kernel, seq0=seq0, ff_chunk=ff_chunk, n_heads=n_heads, dqk=dqk, page=page,
                             ch=ch, n_chunks=n_chunks, lam_init=lam_init)
    row = lambda width: pl.BlockSpec((tm, width), lambda i, pt: (i, 0))
    per_seq_in, per_seq, any_spec, scratch = _stream_specs(q2, page, ch, n_slots, n_heads, dv, lambda i: i, seq0)
    const = lambda x: pl.BlockSpec(x.shape, lambda i, pt: (0,) * x.ndim, pipeline_mode=pl.Buffered(1))
    grid_spec = pltpu.PrefetchScalarGridSpec(
        num_scalar_prefetch=1,
        grid=(n_steps,),
        in_specs=[row(d), row(a2d.shape[1]), row(c2d.shape[1]), const(wout_b), const(gmlp), const(w1_b),
                  const(w2_b), per_seq_in(rows2, seqs_per_step), per_seq_in(t_new, seqs_per_step),
                  per_seq_in(t_new, seqs_per_step)]
        + [const(x) for x in lams] + [const(subln), any_spec, any_spec],
        out_specs=[row(d), per_seq(t_new, seqs_per_step)],
        scratch_shapes=scratch,
    )
    return pl.pallas_call(
        kern,
        grid_spec=grid_spec,
        out_shape=[jax.ShapeDtypeStruct((t, d), F32), jax.ShapeDtypeStruct((n_seqs, t_new, a), F32)],
        compiler_params=pltpu.CompilerParams(dimension_semantics=("arbitrary",),
                                             vmem_limit_bytes=VMEM_LIMIT_BYTES),
        name="finish_sample",
    )(page_table, x2d, a2d, c2d, wout_b, gmlp, w1_b, w2_b, q2, kn, vn, *lams, subln, cache_kt, cache_v)


def kernel(x_prompt, x_sample, cache_k, cache_v, state_conv, page_table, norm_mix, w_in, q_norm, k_norm,
           lambda_q1, lambda_k1, lambda_q2, lambda_k2, subln, conv_w, w_out, norm_mlp, w_ff1, w_ff2):
    depth, n_pool, page, n_heads, _, dqk = cache_k.shape
    dv = cache_v.shape[-1]
    attn_w = n_heads * dv
    conv_width = state_conv.shape[-1]
    conv_k = conv_w.shape[1]
    batch, seq, d_model = x_prompt.shape
    dec_batch, dec_seq, _ = x_sample.shape
    q_scale = dqk ** -0.5
    n_groups = attn_w // dqk

    gmat = (jnp.kron(jnp.eye(n_groups, dtype=F32), jnp.ones((dqk, dqk), F32)) / dqk).astype(BF16)

    xp = x_prompt.reshape(batch * seq, d_model)
    xs = x_sample.reshape(dec_batch * dec_seq, d_model)
    outs = {name: [] for name in ("kp", "vp", "cp", "ks", "vs", "cs")}
    for l in range(depth):
        lam_init = _lambda_init(l)
        win_b = w_in[l].astype(BF16)
        wout_b = w_out[l].astype(BF16)
        w1_b = w_ff1[l].astype(BF16)
        w2_b = w_ff2[l].astype(BF16)
        gmix = norm_mix[l][None]
        gmlp = norm_mlp[l][None]
        gq = jnp.tile(q_norm[l], n_groups)[None]
        gk = jnp.tile(k_norm[l], n_groups)[None]
        lams = [lambda_q1[l][None], lambda_k1[l][None], lambda_q2[l][None], lambda_k2[l][None]]
        sub = subln[l][None]
        cw = conv_w[l]
        pre = dict(attn_w=attn_w, conv_w=conv_width, n_heads=n_heads, q_scale=q_scale)
        fin = functools.partial(_finish, ff_chunk=1024)

        st = state_conv[l]
        hist = jnp.stack([
            jnp.concatenate([st[:, conv_k - 1 - s:, :],
                             jnp.zeros((dec_batch, dec_seq - s, conv_width), F32)], axis=1)
            .reshape(dec_batch * dec_seq, conv_width)
            for s in range(1, conv_k)])
        ts = dec_batch * dec_seq
        qb_s, k, v, kb_s, c_s, u = _prestage_sample(xs, gmix, win_b, gq, gk, gmat, cw, hist, seq_len=dec_seq,
                                                    **pre)
        q2 = jnp.tile(qb_s.reshape(dec_batch, dec_seq, attn_w), (1, 2, 1))
        kn = kb_s.reshape(dec_batch, dec_seq, attn_w)
        vn = v.reshape(dec_batch, dec_seq, attn_w)
        cache_kt = jnp.transpose(cache_k[l], (0, 2, 3, 4, 1)).reshape(n_pool, attn_w, page)
        cache_vr = cache_v[l].reshape(n_pool, page * n_heads, dv)

        tm_p, tk_p, tm_f, tiles_a = 512, 256, 512, 1
        n_pre = batch * seq // tm_p
        n_att = batch * seq // (tk_p * tiles_a)
        assert dec_batch > n_pre + n_att
        share = [dict(seq0=0, n_host=n_pre), dict(seq0=n_pre, n_host=n_att),
                 dict(seq0=n_pre + n_att, n_host=dec_batch - n_pre - n_att)]
        stream_in = (page_table, q2, kn, vn)
        stream_kw = dict(n_heads=n_heads, dqk=dqk, ch=16, lam_init=lam_init)

        wqkt_b = w_in[l][:, :2 * attn_w].T.astype(BF16)
        gqt = jnp.broadcast_to(gq.reshape(attn_w, 1), (attn_w, tm_p))
        gkt = jnp.broadcast_to(gk.reshape(attn_w, 1), (attn_w, tm_p))
        qtb, kt, kb, v_il, vtb, c, c_state, a_s0 = _prestage_prompt(
            xp, gmix, win_b, wqkt_b, gqt, gkt, gmat, cw,
            *stream_in, lams, sub, cache_kt, cache_vr,
            attn_w=attn_w, conv_w=conv_width, q_scale=q_scale, tm=tm_p, tk=tk_p, seq_len=seq, n_slots=2,
            **share[0], **stream_kw)
        outs["kp"].append(jnp.transpose(kt.reshape(batch, n_heads, 2, dqk, seq), (0, 4, 1, 2, 3)))
        outs["vp"].append(v_il.reshape(batch, seq, n_heads, dv))
        outs["cp"].append(c_state)
        a_p, a_s1 = _prompt_attention(qtb, kb.reshape(batch, seq, attn_w), vtb, lams, sub,
                                      *stream_in, cache_kt, cache_vr,
                                      tiles=tiles_a, n_slots=4, **share[1], **stream_kw)
        xp, a_s2 = _finish_sample(xp, a_p.reshape(batch * seq, attn_w), c, wout_b, gmlp, w1_b, w2_b,
                                  *stream_in, lams, sub, cache_kt, cache_vr,
                                  tm=tm_f, ff_chunk=1024, n_slots=2, **share[2], **stream_kw)
        a_s = jnp.concatenate([a_s0, a_s1, a_s2], axis=0)
        xs = fin(xs, a_s.reshape(ts, attn_w), c_s, wout_b, gmlp, w1_b, w2_b, tm=256)
        outs["ks"].append(k.reshape(dec_batch, dec_seq, n_heads, 2, dqk))
        outs["vs"].append(v.reshape(dec_batch, dec_seq, n_heads, dv))
        outs["cs"].append(u.reshape(dec_batch, dec_seq, conv_width)[:, dec_seq - (conv_k - 1):, :])

    return (xp.reshape(batch, seq, d_model), xs.reshape(dec_batch, dec_seq, d_model),
            jnp.stack(outs["kp"]), jnp.stack(outs["vp"]), jnp.stack(outs["cp"]),
            jnp.stack(outs["ks"]), jnp.stack(outs["vs"]), jnp.stack(outs["cs"]))
```

```python
import functools
import math

import jax
import jax.numpy as jnp
from jax import lax
from jax.experimental import pallas as pl
from jax.experimental.pallas import tpu as pltpu

F32 = jnp.float32
BF16 = jnp.bfloat16
EPS = 1e-6
NEG_INF = -1e30

V7X_VMEM_BYTES = 64 * 1024 * 1024
V7X_SUBLANES = 8
VMEM_LIMIT_BYTES = V7X_VMEM_BYTES - 8 * 1024 * 1024


def _lambda_init(layer):
    return 0.8 - 0.6 * math.exp(-0.3 * layer)


def _rms_rows(x, g):
    ms = jnp.mean(x * x, axis=-1, keepdims=True)
    return x * lax.rsqrt(ms + EPS) * g


def _const_spec(shape):
    zeros = (0,) * len(shape)
    return pl.BlockSpec(shape, lambda *_: zeros, pipeline_mode=pl.Buffered(1))


def _lam_value(lq1_ref, lk1_ref, lq2_ref, lk2_ref, lam_init):
    s1 = jnp.sum(lq1_ref[...] * lk1_ref[...], axis=-1, keepdims=True)
    s2 = jnp.sum(lq2_ref[...] * lk2_ref[...], axis=-1, keepdims=True)
    return jnp.exp(s1) - jnp.exp(s2) + lam_init


def _prestage_parts(refs, *, attn_w, conv_w, n_heads, seq_tiles, seq_len, q_scale, sample):
    if sample:
        (x_ref, gmix_ref, win_ref, gq_ref, gk_ref, gmat_ref, cw_ref, hist_ref,
         qb_ref, k_ref, v_ref, kb_ref, c_ref, u_ref, ubuf) = refs
    else:
        (x_ref, gmix_ref, win_ref, wqkt_ref, gqt_ref, gkt_ref, gmat_ref, cw_ref,
         qtb_ref, kt_ref, kb_ref, vil_ref, vtb_ref, c_ref, cs_ref, ubuf) = refs
        wqt_ref, wkt_ref = wqkt_ref.at[0:attn_w], wqkt_ref.at[attn_w:2 * attn_w]
    tm = x_ref.shape[0]
    conv_k = cw_ref.shape[0]
    a, c = attn_w, conv_w
    dv = a // n_heads
    st = {}

    def proj(lo, width):
        return jnp.dot(st["h"], win_ref[:, lo:lo + width], preferred_element_type=F32)

    def group_rms(z, g):
        msq = jnp.dot((z * z).astype(BF16), gmat_ref[...], preferred_element_type=F32)
        return z * lax.rsqrt(msq + EPS) * g

    def group_rms_t(wt_ref, gt_ref):
        zt = lax.dot_general(wt_ref[...], st["h"], (((1,), (1,)), ((), ())), preferred_element_type=F32)
        msq = jnp.dot(gmat_ref[...], (zt * zt).astype(BF16), preferred_element_type=F32)
        return zt * lax.rsqrt(msq + EPS) * gt_ref[...]

    def queries():
        st["h"] = _rms_rows(x_ref[...], gmix_ref[...]).astype(BF16)
        if sample:
            qb_ref[...] = (group_rms(proj(0, a), gq_ref[...]) * q_scale).astype(BF16)
        else:
            qtb_ref[0] = (group_rms_t(wqt_ref, gqt_ref) * q_scale).astype(BF16)

    def keys():
        if sample:
            k = group_rms(proj(a, a), gk_ref[...])
            k_ref[...] = k
            kb_ref[...] = k.astype(BF16)
        else:
            kt = group_rms_t(wkt_ref, gkt_ref)
            kt_ref[0] = kt
            kb_ref[...] = kt.T.astype(BF16)

    def values():
        v = proj(2 * a, a)
        if sample:
            v_ref[...] = v
        else:
            for hh in range(n_heads):
                vil_ref[pl.ds(hh, tm, stride=n_heads), :] = v[:, hh * dv:(hh + 1) * dv]
            vt = v.T
            tk = vtb_ref.shape[-1]
            for t in range(tm // tk):
                vtb_ref[0, t] = vt[:, t * tk:(t + 1) * tk].astype(BF16)

    def conv():
        b_gate = proj(3 * a, c)
        u = proj(3 * a + c, c) * proj(3 * a + 2 * c, c)
        halo = V7X_SUBLANES
        if sample:
            ubuf[0:halo, :] = jnp.zeros((halo, c), F32)
        else:
            first = pl.program_id(0) % seq_tiles == 0

            @pl.when(first)
            def _():
                ubuf[0:halo, :] = jnp.zeros((halo, c), F32)

            @pl.when(jnp.logical_not(first))
            def _():
                ubuf[0:halo, :] = ubuf[tm:tm + halo, :]

        ubuf[halo:halo + tm, :] = u
        if sample:
            t_pos = lax.broadcasted_iota(jnp.int32, (tm, 1), 0) % seq_len
        y = None
        for j in range(conv_k):
            shift = conv_k - 1 - j
            if shift == 0:
                tap = u
            else:
                tap = ubuf[halo - shift:halo - shift + tm, :]
                if sample:
                    tap = jnp.where(t_pos >= shift, tap, hist_ref[shift - 1])
            term = cw_ref[j:j + 1, :] * tap
            y = term if y is None else y + term
        c_ref[...] = (b_gate * y).astype(c_ref.dtype)
        if sample:
            u_ref[...] = u
        else:
            cs_ref[0] = u[tm - (conv_k - 1):tm, :]

    return [queries, keys, values, conv]


def _prestage_kernel(*refs, **kw):
    for part in _prestage_parts(refs, **kw):
        part()


def _prestage_stream_kernel(pt_ref, *refs, n_in, n_out, seq0, dqk, page, ch, n_chunks, lam_init, **kw):
    pre_in, rest = refs[:n_in], refs[n_in:]
    (lq1_ref, lk1_ref, lq2_ref, lk2_ref, sub_ref, q_ref, kn_ref, vn_ref, ckt_hbm, cv_hbm), rest = rest[:10], rest[10:]
    pre_out, (os_ref, ubuf, kbuf, vbuf, ksem, vsem) = rest[:n_out], rest[n_out:]
    step = pl.program_id(0)
    seqs_per_step = q_ref.shape[0]
    n_heads = kw["n_heads"]
    items = lambda s: [(seq0 + s * seqs_per_step + k, c) for k in range(seqs_per_step) for c in range(n_chunks)]
    stream = _SampleStream(pt_ref, ckt_hbm, cv_hbm, kbuf, vbuf, ksem, vsem, items=items,
                           n_steps=pl.num_programs(0), n_heads=n_heads, dqk=dqk,
                           dv=q_ref.shape[-1] // n_heads, page=page, ch=ch, t_new=kn_ref.shape[1])

    @pl.when(step == 0)
    def _():
        stream.prologue()

    parts = _prestage_parts(tuple(pre_in) + tuple(pre_out) + (ubuf,), **kw)
    lam = _lam_value(lq1_ref, lk1_ref, lq2_ref, lk2_ref, lam_init)
    _stream_sequences(stream, step, q_ref, kn_ref, vn_ref, os_ref, lam, sub_ref[...], lam_init, parts)


def _prestage_sample(x2d, gmix, win_b, gq, gk, gmat, cw, hist, *, attn_w, conv_w, n_heads, seq_len, q_scale):
    t, d = x2d.shape
    assert t % seq_len == 0
    args = [x2d, gmix, win_b, gq, gk, gmat, cw, hist]
    out_shape = [jax.ShapeDtypeStruct((t, attn_w), BF16), jax.ShapeDtypeStruct((t, attn_w), F32),
                 jax.ShapeDtypeStruct((t, attn_w), F32), jax.ShapeDtypeStruct((t, attn_w), BF16),
                 jax.ShapeDtypeStruct((t, conv_w), F32), jax.ShapeDtypeStruct((t, conv_w), F32)]
    kern = functools.partial(_prestage_kernel, attn_w=attn_w, conv_w=conv_w, n_heads=n_heads, seq_tiles=1,
                             seq_len=seq_len, q_scale=q_scale, sample=True)
    return pl.pallas_call(
        kern,
        grid=(1,),
        in_specs=[_const_spec(x.shape) for x in args],
        out_specs=[pl.BlockSpec(s.shape, lambda i: (0, 0)) for s in out_shape],
        out_shape=out_shape,
        scratch_shapes=[pltpu.VMEM((t + 2 * V7X_SUBLANES, conv_w), F32)],
        compiler_params=pltpu.CompilerParams(dimension_semantics=("arbitrary",),
                                             vmem_limit_bytes=VMEM_LIMIT_BYTES),
        name="prestage_sample",
    )(*args)


def _prestage_prompt(x2d, gmix, win_b, wqkt_b, gqt, gkt, gmat, cw,
                     page_table, q2, kn, vn, lams, subln, cache_kt, cache_v,
                     *, attn_w, conv_w, n_heads, dqk, tm, tk, seq_len, q_scale, ch, n_slots, lam_init, seq0,
                     n_host):
    t, d = x2d.shape
    conv_k = cw.shape[0]
    dv = attn_w // n_heads
    assert seq_len % tm == 0 and tm % tk == 0 and gkt.shape == (attn_w, tm) and gqt.shape == (attn_w, tm)
    seq_tiles = seq_len // tm
    n_seq = t // seq_len
    n_steps = t // tm
    n_seqs, rows2 = n_host, q2.shape[1]
    t_new = rows2 // 2
    page = cache_kt.shape[-1]
    n_pages = page_table.shape[1]
    assert n_pages % ch == 0 and n_seqs % n_steps == 0
    seqs_per_step = n_seqs // n_steps
    row = lambda width: pl.BlockSpec((tm, width), lambda i, pt: (i, 0))
    col = pl.BlockSpec((1, attn_w, tm), lambda i, pt: (i // seq_tiles, 0, i % seq_tiles))
    const = lambda x: pl.BlockSpec(x.shape, lambda i, pt: (0,) * x.ndim, pipeline_mode=pl.Buffered(1))
    per_seq_in, per_seq, any_spec, scratch = _stream_specs(q2, page, ch, n_slots, n_heads, dv, lambda i: i, seq0)
    pre_args = [x2d, gmix, win_b, wqkt_b, gqt, gkt, gmat, cw]
    stream_args = [*lams, subln, q2, kn, vn, cache_kt, cache_v]
    in_specs = ([row(d)] + [const(x) for x in pre_args[1:]] + [const(x) for x in lams] + [const(subln)]
                + [per_seq_in(rows2, seqs_per_step), per_seq_in(t_new, seqs_per_step),
                   per_seq_in(t_new, seqs_per_step), any_spec, any_spec])
    out_shape = [jax.ShapeDtypeStruct((n_seq, attn_w, seq_len), BF16),
                 jax.ShapeDtypeStruct((n_seq, attn_w, seq_len), F32),
                 jax.ShapeDtypeStruct((t, attn_w), BF16),
                 jax.ShapeDtypeStruct((t * n_heads, dv), F32),
                 jax.ShapeDtypeStruct((n_seq, seq_len // tk, attn_w, tk), BF16),
                 jax.ShapeDtypeStruct((t, conv_w), BF16),
                 jax.ShapeDtypeStruct((n_seq, conv_k - 1, conv_w), F32),
                 jax.ShapeDtypeStruct((n_seqs, t_new, attn_w), F32)]
    out_specs = [col, col, row(attn_w),
                 pl.BlockSpec((tm * n_heads, dv), lambda i, pt: (i, 0)),
                 pl.BlockSpec((1, tm // tk, attn_w, tk), lambda i, pt: (i // seq_tiles, i % seq_tiles, 0, 0)),
                 row(conv_w),
                 pl.BlockSpec((1, conv_k - 1, conv_w), lambda i, pt: (i // seq_tiles, 0, 0)),
                 per_seq(t_new, seqs_per_step)]
    kern = functools.partial(_prestage_stream_kernel, n_in=len(pre_args), n_out=len(out_shape) - 1, seq0=seq0, dqk=dqk,
                             page=page, ch=ch, n_chunks=n_pages // ch, lam_init=lam_init,
                             attn_w=attn_w, conv_w=conv_w, n_heads=n_heads, seq_tiles=seq_tiles,
                             seq_len=seq_len, q_scale=q_scale, sample=False)
    grid_spec = pltpu.PrefetchScalarGridSpec(
        num_scalar_prefetch=1,
        grid=(n_steps,),
        in_specs=in_specs,
        out_specs=out_specs,
        scratch_shapes=[pltpu.VMEM((tm + 2 * V7X_SUBLANES, conv_w), F32)] + scratch,
    )
    return pl.pallas_call(
        kern,
        grid_spec=grid_spec,
        out_shape=out_shape,
        compiler_params=pltpu.CompilerParams(dimension_semantics=("arbitrary",),
                                             vmem_limit_bytes=VMEM_LIMIT_BYTES),
        name="prestage_prompt_sample",
    )(page_table, *pre_args, *stream_args)


def _prompt_attn_parts(i, off, qt_ref, k_ref, vt_ref, sub_ref, o_ref, lam, *, n_heads, dqk, lam_init):
    tq = tk = vt_ref.shape[-1]
    dv = qt_ref.shape[1] // n_heads
    st = {}

    def prepare():
        row = lax.broadcasted_iota(jnp.int32, (dv, tq), 0)
        qst = []
        for h in range(n_heads):
            qh = qt_ref[0, h * dv:(h + 1) * dv, off:off + tq]
            zero = jnp.zeros_like(qh)
            qst.append(jnp.concatenate([jnp.where(row < dqk, qh, zero), jnp.where(row >= dqk, qh, zero)],
                                       axis=1))
        st["qst"] = qst

    def update(j, carry, masked):
        qst = st["qst"]
        start = pl.multiple_of(j * tk, tk)

        def score(h):
            kj = k_ref[0, pl.ds(start, tk), h * dv:(h + 1) * dv]
            s = jnp.dot(kj, qst[h], preferred_element_type=F32)
            if masked:
                key = lax.broadcasted_iota(jnp.int32, s.shape, 0)
                qry = lax.broadcasted_iota(jnp.int32, s.shape, 1) % tq
                s = jnp.where(qry >= key, s, NEG_INF)
            return s

        def softmax(h, s):
            m, l, _ = carry[h]
            m_new = jnp.maximum(m, jnp.max(s, axis=0, keepdims=True))
            p = jnp.exp(s - m_new)
            alpha = jnp.exp(m - m_new)
            l = alpha * l + jnp.sum(p, axis=0, keepdims=True)
            return m_new, l, alpha, p.astype(BF16)

        def weigh(h, m_new, l, alpha, p):
            vtj = vt_ref[0, j, h * dv:(h + 1) * dv, :]
            acc = alpha * carry[h][2] + jnp.dot(vtj, p, preferred_element_type=F32)
            return m_new, l, acc

        scores = [score(h) for h in range(n_heads)]
        probs = [softmax(h, s) for h, s in enumerate(scores)]
        return tuple(weigh(h, *pr) for h, pr in enumerate(probs))

    def below_diagonal():
        init = tuple((jnp.full((1, 2 * tq), NEG_INF, F32), jnp.zeros((1, 2 * tq), F32),
                      jnp.zeros((dv, 2 * tq), F32)) for _ in range(n_heads))
        st["carry"] = lax.fori_loop(0, i, lambda j, c: update(j, c, False), init)

    def diagonal():
        st["carry"] = update(i, st["carry"], True)

    def finish():
        for h in range(n_heads):
            m, l, acc = st["carry"][h]
            ot = acc[:, :tq] / l[:, :tq] - lam * (acc[:, tq:] / l[:, tq:])
            o = _rms_rows(ot.T, sub_ref[...]) * (1.0 - lam_init)
            o_ref[0, off:off + tq, h * dv:(h + 1) * dv] = o.astype(o_ref.dtype)

    return [prepare, below_diagonal, diagonal, finish]


def _run_stream(stream, step, jobs, parts):
    j = 0
    for init, n_items, done in jobs:
        qbd, state = init()
        for _ in range(n_items):
            slot = j % stream.n_slots
            stream.wait(slot)
            state = stream.consume(slot, qbd, state)
            stream.start_after(step, j)
            j += 1
            if parts:
                parts.pop(0)()
        done(state)
    assert j == stream.n_items
    while parts:
        parts.pop(0)()


def _seq_init(stream, k, q_ref, kn_ref, vn_ref):
    return lambda: stream.init_state(q_ref[k].astype(F32), kn_ref[k].astype(F32), vn_ref[k])


def _seq_done(stream, k, os_ref, lam, sub, lam_init):
    dv = stream.dv

    def store(h, o):
        os_ref[k, :, h * dv:(h + 1) * dv] = o

    return lambda state: stream.finalize(state, lam, sub, lam_init, store)


def _partial_store(k, pm_ref, pl_ref, pa_ref):
    def done(state):
        for pair, (m, l, acc) in enumerate(state):
            pm_ref[k, pair] = jnp.broadcast_to(m, pm_ref.shape[2:])
            pl_ref[k, pair] = jnp.broadcast_to(l, pl_ref.shape[2:])
            pa_ref[k, pair] = acc
    return done


def _partial_init(stream, k, q_ref, kn_ref, vn_ref, pm_ref, pl_ref, pa_ref):
    def init():
        qbd, state = _seq_init(stream, k, q_ref, kn_ref, vn_ref)()
        return qbd, [(pm_ref[k, pair][:, 0:1], pl_ref[k, pair][:, 0:1], pa_ref[k, pair])
                     for pair in range(len(state))]
    return init


def _stream_sequences(stream, step, q_ref, kn_ref, vn_ref, os_ref, lam, sub, lam_init, parts):
    n = stream.n_items // q_ref.shape[0]
    jobs = [(_seq_init(stream, k, q_ref, kn_ref, vn_ref), n, _seq_done(stream, k, os_ref, lam, sub, lam_init))
            for k in range(q_ref.shape[0])]
    _run_stream(stream, step, jobs, parts)


def _prompt_attn_kernel(pt_ref, qt_ref, k_ref, vt_ref, lq1_ref, lk1_ref, lq2_ref, lk2_ref, sub_ref,
                        q_ref, kn_ref, vn_ref, ckt_hbm, cv_hbm, o_ref, pm_ref, pl_ref, pa_ref,
                        kbuf, vbuf, ksem, vsem,
                        *, seq0, n_heads, dqk, page, ch, head_chunks, lam_init):
    step = pl.program_id(0) * pl.num_programs(1) + pl.program_id(1)
    n_steps = pl.num_programs(0) * pl.num_programs(1)
    seqs_per_step = q_ref.shape[0]
    items = lambda s: [(seq0 + s * seqs_per_step + k, c) for k in range(seqs_per_step) for c in range(head_chunks)]
    stream = _SampleStream(pt_ref, ckt_hbm, cv_hbm, kbuf, vbuf, ksem, vsem, items=items, n_steps=n_steps,
                           n_heads=n_heads, dqk=dqk, dv=q_ref.shape[-1] // n_heads, page=page, ch=ch,
                           t_new=kn_ref.shape[1])

    @pl.when(step == 0)
    def _():
        stream.prologue()

    lam = _lam_value(lq1_ref, lk1_ref, lq2_ref, lk2_ref, lam_init)
    tq = vt_ref.shape[-1]
    tiles = qt_ref.shape[-1] // tq
    parts = []
    for t in range(tiles):
        parts += _prompt_attn_parts(pl.program_id(1) * tiles + t, t * tq, qt_ref, k_ref, vt_ref, sub_ref, o_ref,
                                    lam, n_heads=n_heads, dqk=dqk, lam_init=lam_init)
    jobs = [(_seq_init(stream, k, q_ref, kn_ref, vn_ref), head_chunks, _partial_store(k, pm_ref, pl_ref, pa_ref))
            for k in range(seqs_per_step)]
    _run_stream(stream, step, jobs, parts)


def _stream_specs(q2, page, ch, n_slots, n_heads, dv, index, seq0):
    a = q2.shape[-1]

    def per_seq(r, n, offset=0):
        assert offset % n == 0
        return pl.BlockSpec((n, r, a), lambda *ids: (index(*ids[:-1]) + offset // n, 0, 0))

    per_seq_in = lambda r, n: per_seq(r, n, seq0)
    any_spec = pl.BlockSpec(memory_space=pl.ANY)
    scratch = [pltpu.VMEM((n_slots, ch, a, page), F32), pltpu.VMEM((n_slots, ch * page * n_heads, dv), F32),
               pltpu.SemaphoreType.DMA((n_slots,)), pltpu.SemaphoreType.DMA((n_slots,))]
    return per_seq_in, per_seq, any_spec, scratch


def _prompt_attention(qtb, kb, vtb, lams, subln, page_table, q2, kn, vn, cache_kt, cache_v,
                      *, tiles, n_heads, dqk, ch, head_chunks, lam_init, seq0, n_host):
    b, a, s = qtb.shape
    nk, tk = vtb.shape[1], vtb.shape[3]
    nq = s // (tk * tiles)
    n_seqs, rows2 = n_host, q2.shape[1]
    t_new = rows2 // 2
    page = cache_kt.shape[-1]
    dv = cache_v.shape[-1]
    n_pages = page_table.shape[1]
    assert n_pages % ch == 0 and n_seqs % (b * nq) == 0 and s % (tk * tiles) == 0
    seqs_per_step = n_seqs // (b * nq)
    kern = functools.partial(_prompt_attn_kernel, seq0=seq0, n_heads=n_heads, dqk=dqk, page=page, ch=ch,
                             head_chunks=head_chunks, lam_init=lam_init)
    per_seq_in, per_seq, any_spec, scratch = _stream_specs(q2, page, ch, head_chunks * seqs_per_step, n_heads, dv,
                                                           lambda bi, i: bi * nq + i, seq0)
    n_pairs, rows4 = n_heads // 2, 2 * rows2
    part = lambda w: pl.BlockSpec((seqs_per_step, n_pairs, rows4, w), lambda bi, i, pt: (bi * nq + i, 0, 0, 0))
    part_shape = lambda w: jax.ShapeDtypeStruct((n_seqs, n_pairs, rows4, w), F32)
    const = lambda x: pl.BlockSpec(x.shape, lambda bi, i, pt: (0,) * x.ndim, pipeline_mode=pl.Buffered(1))
    grid_spec = pltpu.PrefetchScalarGridSpec(
        num_scalar_prefetch=1,
        grid=(b, nq),
        in_specs=[pl.BlockSpec((1, a, tk * tiles), lambda bi, i, pt: (bi, 0, i)),
                  pl.BlockSpec((1, s, a), lambda bi, i, pt: (bi, 0, 0)),
                  pl.BlockSpec((1, nk, a, tk), lambda bi, i, pt: (bi, 0, 0, 0))]
        + [const(x) for x in lams] + [const(subln)]
        + [per_seq_in(rows2, seqs_per_step), per_seq_in(t_new, seqs_per_step), per_seq_in(t_new, seqs_per_step),
           any_spec, any_spec],
        out_specs=[pl.BlockSpec((1, tk * tiles, a), lambda bi, i, pt: (bi, i, 0)),
                   part(dv), part(dv), part(2 * dv)],
        scratch_shapes=scratch,
    )
    return pl.pallas_call(
        kern,
        grid_spec=grid_spec,
        out_shape=[jax.ShapeDtypeStruct((b, s, a), BF16), part_shape(dv), part_shape(dv), part_shape(2 * dv)],
        compiler_params=pltpu.CompilerParams(dimension_semantics=("arbitrary",) * 2,
                                             vmem_limit_bytes=VMEM_LIMIT_BYTES),
        name="prompt_attn_sample",
    )(page_table, qtb, kb, vtb, *lams, subln, q2, kn, vn, cache_kt, cache_v)


def _k_page_copy(ckt_hbm, kbuf, sem, page_idx, slot, p):
    return pltpu.make_async_copy(ckt_hbm.at[page_idx], kbuf.at[slot, p], sem.at[slot])


def _v_page_copy(cv_hbm, vbuf, sem, page_idx, slot, p, rows):
    return pltpu.make_async_copy(cv_hbm.at[page_idx], vbuf.at[slot, pl.ds(p * rows, rows), :], sem.at[slot])


class _SampleStream:
    def __init__(self, pt_ref, ckt_hbm, cv_hbm, kbuf, vbuf, ksem, vsem, *, items, n_steps, n_heads, dqk, dv, page,
                 ch, t_new):
        self.n_slots = kbuf.shape[0]
        self.items, self.n_steps, self.n_items = items, n_steps, len(items(0))
        assert self.n_items % self.n_slots == 0 and n_heads % 2 == 0
        self.pt_ref, self.ckt_hbm, self.cv_hbm = pt_ref, ckt_hbm, cv_hbm
        self.kbuf, self.vbuf, self.ksem, self.vsem = kbuf, vbuf, ksem, vsem
        self.n_heads, self.dqk, self.dv = n_heads, dqk, dv
        self.page, self.ch, self.t_new = page, ch, t_new

    def start(self, row, chunk, slot):
        for p in range(self.ch):
            pg = self.pt_ref[row, chunk * self.ch + p]
            _k_page_copy(self.ckt_hbm, self.kbuf, self.ksem, pg, slot, p).start()
            _v_page_copy(self.cv_hbm, self.vbuf, self.vsem, pg, slot, p, self.page * self.n_heads).start()

    def wait(self, slot):
        for p in range(self.ch):
            _k_page_copy(self.ckt_hbm, self.kbuf, self.ksem, 0, slot, p).wait()
            _v_page_copy(self.cv_hbm, self.vbuf, self.vsem, 0, slot, p, self.page * self.n_heads).wait()

    def prologue(self):
        for j, (row, chunk) in enumerate(self.items(0)[:self.n_slots]):
            self.start(row, chunk, j)

    def start_after(self, step, j):
        slot = j % self.n_slots
        nxt = j + self.n_slots
        if nxt < self.n_items:
            self.start(*self.items(step)[nxt], slot)
        else:
            @pl.when(step + 1 < self.n_steps)
            def _():
                self.start(*self.items(step + 1)[nxt - self.n_items], slot)

    def init_state(self, q2, kn, vn):
        t_new, dv, dqk = self.t_new, self.dv, self.dqk
        rows2 = 2 * t_new
        lane = lax.broadcasted_iota(jnp.int32, (rows2, dv), 1)
        rowi = lax.broadcasted_iota(jnp.int32, (rows2, dv), 0)
        lane_lo = jnp.where(rowi < t_new, 0, dqk)
        comp_mask = jnp.logical_and(lane >= lane_lo, lane < lane_lo + dqk)
        t_row = lax.broadcasted_iota(jnp.int32, (rows2, 1), 0) % t_new
        zero = jnp.zeros((rows2, dv), F32)
        qbd, state = [], []
        for pair in range(self.n_heads // 2):
            per_head = []
            for h in (2 * pair, 2 * pair + 1):
                qf = jnp.where(comp_mask, q2[:, h * dv:(h + 1) * dv], 0.0)
                s_new = [jnp.sum(qf * kn[j:j + 1, h * dv:(h + 1) * dv], axis=-1, keepdims=True)
                         for j in range(t_new)]
                valid = [t_row >= j for j in range(t_new)]
                m = s_new[0]
                for j in range(1, t_new):
                    m = jnp.maximum(m, jnp.where(valid[j], s_new[j], NEG_INF))
                l = jnp.zeros((rows2, 1), F32)
                acc = zero
                for j in range(t_new):
                    pj = jnp.where(valid[j], jnp.exp(s_new[j] - m), 0.0)
                    l = l + pj
                    acc = acc + pj * vn[j:j + 1, h * dv:(h + 1) * dv]
                per_head.append((qf, m, l, acc))
            (q0, m0, l0, a0), (q1, m1, l1, a1) = per_head
            qbd.append(jnp.concatenate([jnp.concatenate([q0, zero], axis=1),
                                        jnp.concatenate([zero, q1], axis=1)], axis=0))
            state.append((jnp.concatenate([m0, m1], axis=0), jnp.concatenate([l0, l1], axis=0),
                          jnp.concatenate([jnp.concatenate([a0, zero], axis=1),
                                           jnp.concatenate([zero, a1], axis=1)], axis=0)))
        return qbd, state

    def consume(self, slot, qbd, state):
        dv = self.dv
        tokens = self.ch * self.page
        scores = []
        for pair in range(len(state)):
            kt = jnp.concatenate([self.kbuf[slot, p, 2 * pair * dv:(2 * pair + 2) * dv, :]
                                  for p in range(self.ch)], axis=1)
            scores.append(jnp.dot(qbd[pair], kt, preferred_element_type=F32))
        probs = []
        for s, (m, l, _) in zip(scores, state):
            m_new = jnp.maximum(m, jnp.max(s, axis=-1, keepdims=True))
            p = jnp.exp(s - m_new)
            alpha = jnp.exp(m - m_new)
            probs.append((m_new, alpha * l + jnp.sum(p, axis=-1, keepdims=True), alpha, p))
        out = []
        for pair, (m_new, l, alpha, p) in enumerate(probs):
            h0, h1 = 2 * pair, 2 * pair + 1
            v2 = jnp.concatenate([self.vbuf[slot, pl.ds(h0, tokens, stride=self.n_heads), :],
                                  self.vbuf[slot, pl.ds(h1, tokens, stride=self.n_heads), :]], axis=1)
            acc = alpha * state[pair][2] + jnp.dot(p, v2, preferred_element_type=F32)
            out.append((m_new, l, acc))
        return out

    def finalize(self, state, lam, sub, lam_init, store):
        t_new, dv = self.t_new, self.dv
        rows2 = 2 * t_new
        for pair, (m, l, acc) in enumerate(state):
            for k in range(2):
                a_h = acc[k * rows2:(k + 1) * rows2, k * dv:(k + 1) * dv]
                l_h = l[k * rows2:(k + 1) * rows2]
                o = a_h[:t_new] / l_h[:t_new] - lam * (a_h[t_new:] / l_h[t_new:])
                store(2 * pair + k, _rms_rows(o, sub) * (1.0 - lam_init))


def _finish_parts(x_ref, a_ref, c_ref, wout_ref, gmlp_ref, w1_ref, w2_ref, y_ref, ff_chunk):
    a_w = a_ref.shape[1]
    d_ff = w1_ref.shape[1]
    st = {}

    def project():
        mix = jnp.dot(a_ref[...].astype(BF16), wout_ref[0:a_w, :], preferred_element_type=F32)
        mix = mix + jnp.dot(c_ref[...].astype(BF16), wout_ref[a_w:, :], preferred_element_type=F32)
        st["acc"] = x_ref[...] + mix
        st["h"] = _rms_rows(st["acc"], gmlp_ref[...]).astype(BF16)

    def ff(lo):
        z = jnp.dot(st["h"], w1_ref[:, lo:lo + ff_chunk], preferred_element_type=F32)
        z = jnp.maximum(z, 0.0)
        z = (z * z).astype(BF16)
        st["acc"] = st["acc"] + jnp.dot(z, w2_ref[lo:lo + ff_chunk, :], preferred_element_type=F32)

    def store():
        y_ref[...] = st["acc"]

    return [project] + [functools.partial(ff, lo) for lo in range(0, d_ff, ff_chunk)] + [store]


def _finish_kernel(x_ref, a_ref, c_ref, wout_ref, gmlp_ref, w1_ref, w2_ref, y_ref, *, ff_chunk):
    for part in _finish_parts(x_ref, a_ref, c_ref, wout_ref, gmlp_ref, w1_ref, w2_ref, y_ref, ff_chunk):
        part()


def _finish_sample_kernel(pt_ref, x_ref, a_ref, c_ref, wout_ref, gmlp_ref, w1_ref, w2_ref,
                          q_ref, kn_ref, vn_ref, qt_ref, knt_ref, vnt_ref, pm_ref, pl_ref, pa_ref,
                          lq1_ref, lk1_ref, lq2_ref, lk2_ref, sub_ref, ckt_hbm, cv_hbm,
                          y_ref, os_ref, ost_ref, kbuf, vbuf, ksem, vsem,
                          *, seq0, tail_seq0, tail_chunk0, ff_chunk, n_heads, dqk, page, ch, n_chunks, lam_init):
    step = pl.program_id(0)
    seqs_per_step, tails_per_step = q_ref.shape[0], qt_ref.shape[0]
    t_new = kn_ref.shape[1]
    dv = q_ref.shape[-1] // n_heads
    n_tail = n_chunks - tail_chunk0
    items = lambda s: ([(seq0 + s * seqs_per_step + k, c) for k in range(seqs_per_step) for c in range(n_chunks)]
                       + [(tail_seq0 + s * tails_per_step + k, c) for k in range(tails_per_step)
                          for c in range(tail_chunk0, n_chunks)])
    stream = _SampleStream(pt_ref, ckt_hbm, cv_hbm, kbuf, vbuf, ksem, vsem, items=items,
                           n_steps=pl.num_programs(0), n_heads=n_heads, dqk=dqk, dv=dv,
                           page=page, ch=ch, t_new=t_new)

    @pl.when(step == 0)
    def _():
        stream.prologue()

    parts = _finish_parts(x_ref, a_ref, c_ref, wout_ref, gmlp_ref, w1_ref, w2_ref, y_ref, ff_chunk)
    lam = _lam_value(lq1_ref, lk1_ref, lq2_ref, lk2_ref, lam_init)
    sub = sub_ref[...]
    jobs = [(_seq_init(stream, k, q_ref, kn_ref, vn_ref), n_chunks, _seq_done(stream, k, os_ref, lam, sub, lam_init))
            for k in range(seqs_per_step)]
    jobs += [(_partial_init(stream, k, qt_ref, knt_ref, vnt_ref, pm_ref, pl_ref, pa_ref), n_tail,
              _seq_done(stream, k, ost_ref, lam, sub, lam_init)) for k in range(tails_per_step)]
    _run_stream(stream, step, jobs, parts)


def _finish(x2d, a2d, c2d, wout_b, gmlp, w1_b, w2_b, *, tm, ff_chunk):
    t, d = x2d.shape
    row = lambda width: pl.BlockSpec((tm, width), lambda i: (i, 0))
    return pl.pallas_call(
        functools.partial(_finish_kernel, ff_chunk=ff_chunk),
        grid=(t // tm,),
        in_specs=[row(d), row(a2d.shape[1]), row(c2d.shape[1]), _const_spec(wout_b.shape),
                  _const_spec(gmlp.shape), _const_spec(w1_b.shape), _const_spec(w2_b.shape)],
        out_specs=row(d),
        out_shape=jax.ShapeDtypeStruct((t, d), F32),
        compiler_params=pltpu.CompilerParams(dimension_semantics=("arbitrary",),
                                             vmem_limit_bytes=VMEM_LIMIT_BYTES),
        name="finish",
    )(x2d, a2d, c2d, wout_b, gmlp, w1_b, w2_b)


def _finish_sample(x2d, a2d, c2d, wout_b, gmlp, w1_b, w2_b, page_table, q2, kn, vn, lams, subln, cache_kt, cache_v,
                   tail_state, *, tm, ff_chunk, n_heads, dqk, ch, n_slots, lam_init, seq0, n_host, tail_seq0,
                   tail_chunk0):
    t, d = x2d.shape
    n_seqs, rows2, a = n_host, q2.shape[1], q2.shape[2]
    t_new = rows2 // 2
    n_pool, _, page = cache_kt.shape
    dv = cache_v.shape[-1]
    n_pages = page_table.shape[1]
    n_steps = t // tm
    assert n_pages % ch == 0 and n_seqs % n_steps == 0
    n_chunks = n_pages // ch
    seqs_per_step = n_seqs // n_steps
    n_tail = tail_state[0].shape[0]
    assert n_tail % n_steps == 0
    tails_per_step = n_tail // n_steps
    kern = functools.partial(_finish_sample_kernel, seq0=seq0, tail_seq0=tail_seq0, tail_chunk0=tail_chunk0,
                             ff_chunk=ff_chunk, n_heads=n_heads, dqk=dqk, page=page,
                             ch=ch, n_chunks=n_chunks, lam_init=lam_init)
    row = lambda width: pl.BlockSpec((tm, width), lambda i, pt: (i, 0))
    per_seq_in, per_seq, any_spec, scratch = _stream_specs(q2, page, ch, n_slots, n_heads, dv, lambda i: i, seq0)
    tail_in = lambda r: per_seq(r, tails_per_step, tail_seq0)
    part = lambda x: pl.BlockSpec((tails_per_step,) + x.shape[1:], lambda i, pt: (i, 0, 0, 0))
    const = lambda x: pl.BlockSpec(x.shape, lambda i, pt: (0,) * x.ndim, pipeline_mode=pl.Buffered(1))
    grid_spec = pltpu.PrefetchScalarGridSpec(
        num_scalar_prefetch=1,
        grid=(n_steps,),
        in_specs=[row(d), row(a2d.shape[1]), row(c2d.shape[1]), const(wout_b), const(gmlp), const(w1_b),
                  const(w2_b), per_seq_in(rows2, seqs_per_step), per_seq_in(t_new, seqs_per_step),
                  per_seq_in(t_new, seqs_per_step), tail_in(rows2), tail_in(t_new), tail_in(t_new)]
        + [part(x) for x in tail_state]
        + [const(x) for x in lams] + [const(subln), any_spec, any_spec],
        out_specs=[row(d), per_seq(t_new, seqs_per_step), per_seq(t_new, tails_per_step)],
        scratch_shapes=scratch,
    )
    return pl.pallas_call(
        kern,
        grid_spec=grid_spec,
        out_shape=[jax.ShapeDtypeStruct((t, d), F32), jax.ShapeDtypeStruct((n_seqs, t_new, a), F32),
                   jax.ShapeDtypeStruct((n_tail, t_new, a), F32)],
        compiler_params=pltpu.CompilerParams(dimension_semantics=("arbitrary",),
                                             vmem_limit_bytes=VMEM_LIMIT_BYTES),
        name="finish_sample",
    )(page_table, x2d, a2d, c2d, wout_b, gmlp, w1_b, w2_b, q2, kn, vn, q2, kn, vn, *tail_state, *lams, subln,
      cache_kt, cache_v)


def kernel(x_prompt, x_sample, cache_k, cache_v, state_conv, page_table, norm_mix, w_in, q_norm, k_norm,
           lambda_q1, lambda_k1, lambda_q2, lambda_k2, subln, conv_w, w_out, norm_mlp, w_ff1, w_ff2):
    depth, n_pool, page, n_heads, _, dqk = cache_k.shape
    dv = cache_v.shape[-1]
    attn_w = n_heads * dv
    conv_width = state_conv.shape[-1]
    conv_k = conv_w.shape[1]
    batch, seq, d_model = x_prompt.shape
    dec_batch, dec_seq, _ = x_sample.shape
    q_scale = dqk ** -0.5
    n_groups = attn_w // dqk

    gmat = (jnp.kron(jnp.eye(n_groups, dtype=F32), jnp.ones((dqk, dqk), F32)) / dqk).astype(BF16)

    xp = x_prompt.reshape(batch * seq, d_model)
    xs = x_sample.reshape(dec_batch * dec_seq, d_model)
    outs = {name: [] for name in ("kp", "vp", "cp", "ks", "vs", "cs")}
    for l in range(depth):
        lam_init = _lambda_init(l)
        win_b = w_in[l].astype(BF16)
        wout_b = w_out[l].astype(BF16)
        w1_b = w_ff1[l].astype(BF16)
        w2_b = w_ff2[l].astype(BF16)
        gmix = norm_mix[l][None]
        gmlp = norm_mlp[l][None]
        gq = jnp.tile(q_norm[l], n_groups)[None]
        gk = jnp.tile(k_norm[l], n_groups)[None]
        lams = [lambda_q1[l][None], lambda_k1[l][None], lambda_q2[l][None], lambda_k2[l][None]]
        sub = subln[l][None]
        cw = conv_w[l]
        pre = dict(attn_w=attn_w, conv_w=conv_width, n_heads=n_heads, q_scale=q_scale)
        fin = functools.partial(_finish, ff_chunk=1024)

        st = state_conv[l]
        hist = jnp.stack([
            jnp.concatenate([st[:, conv_k - 1 - s:, :],
                             jnp.zeros((dec_batch, dec_seq - s, conv_width), F32)], axis=1)
            .reshape(dec_batch * dec_seq, conv_width)
            for s in range(1, conv_k)])
        ts = dec_batch * dec_seq
        qb_s, k, v, kb_s, c_s, u = _prestage_sample(xs, gmix, win_b, gq, gk, gmat, cw, hist, seq_len=dec_seq,
                                                    **pre)
        q2 = jnp.tile(qb_s.reshape(dec_batch, dec_seq, attn_w), (1, 2, 1))
        kn = kb_s.reshape(dec_batch, dec_seq, attn_w)
        vn = v.reshape(dec_batch, dec_seq, attn_w)
        cache_kt = jnp.transpose(cache_k[l], (0, 2, 3, 4, 1)).reshape(n_pool, attn_w, page)
        cache_vr = cache_v[l].reshape(n_pool, page * n_heads, dv)

        tm_p, tk_p, tm_f, tiles_a = 512, 256, 512, 1
        n_pre = batch * seq // tm_p
        n_att = batch * seq // (tk_p * tiles_a)
        assert dec_batch > n_pre + n_att
        share = [dict(seq0=0, n_host=n_pre), dict(seq0=n_pre, n_host=n_att),
                 dict(seq0=n_pre + n_att, n_host=dec_batch - n_pre - n_att)]
        stream_in = (page_table, q2, kn, vn)
        stream_kw = dict(n_heads=n_heads, dqk=dqk, ch=16, lam_init=lam_init)

        wqkt_b = w_in[l][:, :2 * attn_w].T.astype(BF16)
        gqt = jnp.broadcast_to(gq.reshape(attn_w, 1), (attn_w, tm_p))
        gkt = jnp.broadcast_to(gk.reshape(attn_w, 1), (attn_w, tm_p))
        qtb, kt, kb, v_il, vtb, c, c_state, a_s0 = _prestage_prompt(
            xp, gmix, win_b, wqkt_b, gqt, gkt, gmat, cw,
            *stream_in, lams, sub, cache_kt, cache_vr,
            attn_w=attn_w, conv_w=conv_width, q_scale=q_scale, tm=tm_p, tk=tk_p, seq_len=seq, n_slots=2,
            **share[0], **stream_kw)
        outs["kp"].append(jnp.transpose(kt.reshape(batch, n_heads, 2, dqk, seq), (0, 4, 1, 2, 3)))
        outs["vp"].append(v_il.reshape(batch, seq, n_heads, dv))
        outs["cp"].append(c_state)
        head_chunks = 3
        a_p, *tail_state = _prompt_attention(qtb, kb.reshape(batch, seq, attn_w), vtb, lams, sub,
                                             *stream_in, cache_kt, cache_vr,
                                             tiles=tiles_a, head_chunks=head_chunks, **share[1], **stream_kw)
        xp, a_s2, a_s1 = _finish_sample(xp, a_p.reshape(batch * seq, attn_w), c, wout_b, gmlp, w1_b, w2_b,
                                        *stream_in, lams, sub, cache_kt, cache_vr, tail_state,
                                        tm=tm_f, ff_chunk=1024, n_slots=2, tail_seq0=share[1]["seq0"],
                                        tail_chunk0=head_chunks, **share[2], **stream_kw)
        a_s = jnp.concatenate([a_s0, a_s1, a_s2], axis=0)
        xs = fin(xs, a_s.reshape(ts, attn_w), c_s, wout_b, gmlp, w1_b, w2_b, tm=256)
        outs["ks"].append(k.reshape(dec_batch, dec_seq, n_heads, 2, dqk))
        outs["vs"].append(v.reshape(dec_batch, dec_seq, n_heads, dv))
        outs["cs"].append(u.reshape(dec_batch, dec_seq, conv_width)[:, dec_seq - (conv_k - 1):, :])

    return (xp.reshape(batch, seq, d_model), xs.reshape(dec_batch, dec_seq, d_model),
            jnp.stack(outs["kp"]), jnp.stack(outs["vp"]), jnp.stack(outs["cp"]),
            jnp.stack(outs["ks"]), jnp.stack(outs["vs"]), jnp.stack(outs["cs"]))
```
